```python
import math
import jax, jax.numpy as jnp
from jax import lax
import numpy as np

D_MODEL = 1024
BATCH = 16
SEQ = 2048
DEPTH = 1

PLE_DIM = 256
D_MIX = D_MODEL
M_HEADS = 4
M_DV = (D_MIX // 2) // M_HEADS
M_DK = M_DV // 2
M_CHUNK = 64
CONV_K = 3
A_HEADS = 4
A_DV = (D_MIX // 2) // A_HEADS
A_DK = A_DV // 2
Q_BLOCK = 128
D_FF = 2816
EPS = 1e-6

M_QK = M_HEADS * M_DK
M_V = M_HEADS * M_DV
M_GATES = 4 * M_HEADS
A_QK = A_HEADS * 2 * A_DK
A_V = A_HEADS * A_DV
SPLITS = (M_QK, M_QK, M_V, M_V, M_GATES, A_QK, A_QK, A_V)
D_IN = sum(SPLITS)

kernel_name = "hybrid_mlstm_diffattn_macaron_encoder"


def rmsnorm(x, g):
    xf = x.astype(jnp.float32)
    y = xf * lax.rsqrt(jnp.mean(xf * xf, axis=-1, keepdims=True) + EPS) * g.astype(jnp.float32)
    return y.astype(x.dtype)


def head_rms(x):
    return x * lax.rsqrt(jnp.mean(x * x, axis=-1, keepdims=True) + EPS)


def swiglu(x, w_in, w_out):
    gu = x @ w_in
    g, u = jnp.split(gu, 2, axis=-1)
    return (jax.nn.silu(g) * u) @ w_out


def centred_dwconv(x, w):
    c = x.shape[-1]
    pad = (CONV_K - 1) // 2
    return lax.conv_general_dilated(
        x, w.astype(x.dtype)[:, None, :], window_strides=(1,), padding=((pad, pad),),
        dimension_numbers=('NWC', 'WIO', 'NWC'), feature_group_count=c)


def mlstm_chunkwise(q, k, v, log_i, f_pre):
    B, H, S, DK = q.shape
    DV = v.shape[-1]
    L = M_CHUNK
    NC = S // L
    log_f = jax.nn.log_sigmoid(f_pre)

    def chunks(t):
        return jnp.moveaxis(t.reshape((B, H, NC, L) + t.shape[3:]), 2, 0)

    xs = (chunks(q), chunks(k), chunks(v), chunks(log_i), chunks(log_f))
    tril = jnp.tril(jnp.ones((L, L), dtype=bool))

    def step(carry, inp):
        C, n, m = carry
        qc, kc, vc, li, lf = inp
        b = jnp.cumsum(lf, axis=-1)
        d = jnp.where(tril, b[..., :, None] - b[..., None, :] + li[..., None, :], -jnp.inf)
        inter = b + m[..., None]
        mj = jnp.maximum(inter, jnp.max(d, axis=-1))
        s = jnp.einsum('bhld,bhsd->bhls', qc, kc) * jnp.exp(d - mj[..., None])
        w_inter = jnp.exp(inter - mj)
        num = (w_inter[..., None] * jnp.einsum('bhld,bhdv->bhlv', qc, C)
               + jnp.einsum('bhls,bhsv->bhlv', s, vc))
        den = w_inter * jnp.einsum('bhld,bhd->bhl', qc, n) + jnp.sum(s, axis=-1)
        hc = num / jnp.maximum(jnp.abs(den), jnp.exp(-mj))[..., None]
        bl = b[..., -1]
        g = bl[..., None] - b + li
        m_new = jnp.maximum(bl + m, jnp.max(g, axis=-1))
        decay = jnp.exp(bl + m - m_new)
        wk = jnp.exp(g - m_new[..., None])
        C_new = decay[..., None, None] * C + jnp.einsum('bhsd,bhsv->bhdv', kc * wk[..., None], vc)
        n_new = decay[..., None] * n + jnp.einsum('bhs,bhsd->bhd', wk, kc)
        return (C_new, n_new, m_new), hc

    init = (jnp.zeros((B, H, DK, DV), jnp.float32),
            jnp.zeros((B, H, DK), jnp.float32),
            jnp.zeros((B, H), jnp.float32))
    _, hs = lax.scan(step, init, xs)
    return jnp.moveaxis(hs, 0, 2).reshape(B, H, S, DV)


def mlstm_group(q, k, v, o_pre, gates, conv_w, b_gate, norm_g):
    B, S, _ = q.shape
    dt = q.dtype
    qk = jax.nn.silu(centred_dwconv(jnp.concatenate([q, k], axis=-1), conv_w))
    q, k = jnp.split(qk, 2, axis=-1)
    qh = q.astype(jnp.float32).reshape(B, S, M_HEADS, M_DK).transpose(0, 2, 1, 3)
    kh = (k.astype(jnp.float32) * (M_DK ** -0.5)).reshape(B, S, M_HEADS, M_DK).transpose(0, 2, 1, 3)
    vh = v.astype(jnp.float32).reshape(B, S, M_HEADS, M_DV).transpose(0, 2, 1, 3)
    gt = (gates.astype(jnp.float32) + b_gate.astype(jnp.float32)).reshape(B, S, 4, M_HEADS)
    gt = gt.transpose(2, 0, 3, 1)
    i_fw, f_fw, i_bw, f_bw = gt[0], gt[1], gt[2], gt[3]
    h_fw = mlstm_chunkwise(qh, kh, vh, i_fw, f_fw)
    flip = lambda t: jnp.flip(t, axis=2)
    h_bw = flip(mlstm_chunkwise(flip(qh), flip(kh), flip(vh), jnp.flip(i_bw, -1), jnp.flip(f_bw, -1)))
    h = head_rms(h_fw + h_bw)
    h = h.transpose(0, 2, 1, 3).reshape(B, S, M_V) * norm_g.astype(jnp.float32)
    return (jax.nn.sigmoid(o_pre.astype(jnp.float32)) * h).astype(dt)


def diff_attn_group(q, k, v, lam_q1, lam_k1, lam_q2, lam_k2, norm_g, lam_init):
    B, S, _ = q.shape
    dt = v.dtype
    qh = q.reshape(B, S, A_HEADS, 2, A_DK).transpose(0, 2, 3, 1, 4)
    kh = k.reshape(B, S, A_HEADS, 2, A_DK).transpose(0, 2, 3, 1, 4)
    vh = v.reshape(B, S, A_HEADS, A_DV).transpose(0, 2, 1, 3)
    scale = A_DK ** -0.5
    slopes = jnp.asarray(np.array([2.0 ** (-8.0 * (h + 1) / A_HEADS) for h in range(A_HEADS)],
                                  dtype=np.float32))
    f32 = jnp.float32
    lam = (jnp.exp(jnp.sum(lam_q1.astype(f32) * lam_k1.astype(f32)))
           - jnp.exp(jnp.sum(lam_q2.astype(f32) * lam_k2.astype(f32))) + lam_init)
    NB = S // Q_BLOCK
    qb = jnp.moveaxis(qh.reshape(B, A_HEADS, 2, NB, Q_BLOCK, A_DK), 3, 0)
    kpos = jnp.arange(S, dtype=jnp.int32)
    starts = jnp.arange(NB, dtype=jnp.int32) * Q_BLOCK

    def block(args):
        qblk, start = args
        s = jnp.einsum('bhcqd,bhckd->bhcqk', qblk, kh).astype(f32) * scale
        qpos = start + jnp.arange(Q_BLOCK, dtype=jnp.int32)
        dist = jnp.abs(qpos[:, None] - kpos[None, :]).astype(f32)
        s = s - slopes[None, :, None, None, None] * dist
        pr = jax.nn.softmax(s, axis=-1)
        a = pr[:, :, 0] - lam * pr[:, :, 1]
        return jnp.einsum('bhqk,bhkd->bhqd', a.astype(dt), vh)

    o = lax.map(block, (qb, starts))
    o = jnp.moveaxis(o, 0, 2).reshape(B, A_HEADS, S, A_DV)
    o = head_rms(o.astype(f32)) * (1.0 - lam_init)
    o = o.transpose(0, 2, 1, 3).reshape(B, S, A_V) * norm_g.astype(f32)
    return o.astype(dt)


def setup_inputs(seed: int = 0) -> dict:
    key = jax.random.key(seed)
    ks = jax.random.split(key, 24)
    f32 = jnp.float32
    nrm = lambda k, shape, s: jax.random.normal(k, shape, f32) * s
    gain = lambda k, shape: 1.0 + 0.05 * jax.random.normal(k, shape, f32)
    gate_noise = 0.1 * jax.random.normal(ks[7], (DEPTH, M_GATES), f32)
    f_bias = jnp.linspace(3.0, 6.0, M_HEADS, dtype=f32)
    zero_h = jnp.zeros((M_HEADS,), f32)
    b_mgate = gate_noise + jnp.concatenate([zero_h, f_bias, zero_h, f_bias])[None, :]
    return {
        "x": jax.random.normal(ks[0], (BATCH, SEQ, D_MODEL), f32),
        "p": jax.random.normal(ks[1], (DEPTH, BATCH, SEQ, PLE_DIM), f32),
        "g_ffn1": gain(ks[2], (DEPTH, D_MODEL)),
        "w_ffn1_in": nrm(ks[3], (DEPTH, D_MODEL, 2 * D_FF), D_MODEL ** -0.5),
        "w_ffn1_out": nrm(ks[4], (DEPTH, D_FF, D_MODEL), D_FF ** -0.5),
        "g_mix": gain(ks[5], (DEPTH, D_MODEL)),
        "w_in": nrm(ks[6], (DEPTH, D_MODEL, D_IN), D_MODEL ** -0.5),
        "b_mgate": b_mgate,
        "conv_w": nrm(ks[8], (DEPTH, CONV_K, 2 * M_QK), CONV_K ** -0.5),
        "g_mnorm": gain(ks[9], (DEPTH, M_V)),
        "lam_q1": nrm(ks[10], (DEPTH, A_DK), 0.1),
        "lam_k1": nrm(ks[11], (DEPTH, A_DK), 0.1),
        "lam_q2": nrm(ks[12], (DEPTH, A_DK), 0.1),
        "lam_k2": nrm(ks[13], (DEPTH, A_DK), 0.1),
        "g_anorm": gain(ks[14], (DEPTH, A_V)),
        "w_out": nrm(ks[15], (DEPTH, D_MIX, D_MODEL), D_MIX ** -0.5),
        "g_ffn2": gain(ks[16], (DEPTH, D_MODEL)),
        "w_ffn2_in": nrm(ks[17], (DEPTH, D_MODEL, 2 * D_FF), D_MODEL ** -0.5),
        "w_ffn2_out": nrm(ks[18], (DEPTH, D_FF, D_MODEL), D_FF ** -0.5),
        "g_ple": gain(ks[19], (DEPTH, D_MODEL)),
        "w_ple_gate": nrm(ks[20], (DEPTH, D_MODEL, D_MODEL), D_MODEL ** -0.5),
        "w_ple_proj": nrm(ks[21], (DEPTH, PLE_DIM, D_MODEL), PLE_DIM ** -0.5),
        "g_final": gain(ks[22], (D_MODEL,)),
    }


def reference(x, p, g_ffn1, w_ffn1_in, w_ffn1_out, g_mix, w_in, b_mgate, conv_w, g_mnorm,
              lam_q1, lam_k1, lam_q2, lam_k2, g_anorm, w_out, g_ffn2, w_ffn2_in, w_ffn2_out,
              g_ple, w_ple_gate, w_ple_proj, g_final):
    h = x
    bounds = np.cumsum(np.array(SPLITS))[:-1].tolist()
    for i in range(DEPTH):
        lam_init = 0.8 - 0.6 * math.exp(-0.3 * i)
        h = h + 0.5 * swiglu(rmsnorm(h, g_ffn1[i]), w_ffn1_in[i], w_ffn1_out[i])
        u = rmsnorm(h, g_mix[i])
        z = u @ w_in[i]
        mq, mk, mv, mo, mg, aq, ak, av = jnp.split(z, bounds, axis=-1)
        y_m = mlstm_group(mq, mk, mv, mo, mg, conv_w[i], b_mgate[i], g_mnorm[i])
        y_a = diff_attn_group(aq, ak, av, lam_q1[i], lam_k1[i], lam_q2[i], lam_k2[i],
                              g_anorm[i], lam_init)
        h = h + jnp.concatenate([y_m, y_a], axis=-1) @ w_out[i]
        h = h + 0.5 * swiglu(rmsnorm(h, g_ffn2[i]), w_ffn2_in[i], w_ffn2_out[i])
        gate = jax.nn.sigmoid((rmsnorm(h, g_ple[i]) @ w_ple_gate[i]).astype(jnp.float32))
        h = h + (gate * (p[i] @ w_ple_proj[i]).astype(jnp.float32)).astype(h.dtype)
    return rmsnorm(h, g_final)
```

```python
import functools
import math

import jax
import jax.numpy as jnp
import numpy as np
from jax import lax
from jax.experimental import pallas as pl
from jax.experimental.pallas import tpu as pltpu

F32 = jnp.float32
BF16 = jnp.bfloat16

EPS = 1e-6
LOG2E = 1.4426950408889634

M_HEADS = 4
M_DK = 64
M_DV = 128
A_HEADS = 4
A_DK = 64
A_DV = 128
CONV_K = 3

LANES_V7X = 128
SUBLANES_V7X = 8
VMEM_LIMIT_BYTES_V7X = 56 * 1024 * 1024

TOKEN_TILE = 512
FF_CHUNK = 256
MLSTM_CHUNK = 256
ATTN_Q_TILE = 256


def _dot(a, b):
    return jnp.dot(a, b, preferred_element_type=F32)


def _dot_nt(a, b):
    return lax.dot_general(a, b, (((1,), (1,)), ((), ())), preferred_element_type=F32)


def _rmsnorm(x, g):
    ms = jnp.mean(x * x, axis=-1, keepdims=True)
    return x * lax.rsqrt(ms + EPS) * g


def _swiglu(xn, wg_ref, wu_ref, wo_ref):
    acc = None
    for j in range(wg_ref.shape[0]):
        g = _dot(xn, wg_ref[j])
        u = _dot(xn, wu_ref[j])
        a = (g * jax.nn.sigmoid(g) * u).astype(BF16)
        d = _dot(a, wo_ref[j])
        acc = d if acc is None else acc + d
    return acc


def _pre_kernel(x_ref, g1_ref, wg_ref, wu_ref, wo_ref, gm_ref, wm_ref, wa_ref, wgt_ref,
                h_ref, zm_ref, za_ref, zgi_ref, zgf_ref):
    x = x_ref[...]
    xn = _rmsnorm(x, g1_ref[...]).astype(BF16)
    h = x + 0.5 * _swiglu(xn, wg_ref, wu_ref, wo_ref)
    h_ref[...] = h
    u = _rmsnorm(h, gm_ref[...]).astype(BF16)
    zm_ref[...] = _dot(u, wm_ref[...]).astype(BF16)
    za_ref[...] = _dot(u, wa_ref[...]).astype(BF16)
    zg = _dot(u, wgt_ref[...])
    zgi_ref[...] = zg[:, :LANES_V7X]
    zgf_ref[...] = zg[:, LANES_V7X:]


def _resident(shape):
    nd = len(shape)
    return pl.BlockSpec(shape, lambda *_: (0,) * nd, pipeline_mode=pl.Buffered(1))


def _pre_call(h, g1, wg3, wu3, wo3, gm, wm, wa, wgt):
    t, d = h.shape
    tm = TOKEN_TILE
    row = lambda n: pl.BlockSpec((tm, n), lambda i: (i, 0))
    return pl.pallas_call(
        _pre_kernel,
        grid=(t // tm,),
        in_specs=[row(d), _resident(g1.shape), _resident(wg3.shape), _resident(wu3.shape),
                  _resident(wo3.shape), _resident(gm.shape), _resident(wm.shape),
                  _resident(wa.shape), _resident(wgt.shape)],
        out_specs=[row(d), row(wm.shape[1]), row(wa.shape[1]), row(LANES_V7X), row(LANES_V7X)],
        out_shape=[jax.ShapeDtypeStruct((t, d), F32),
                   jax.ShapeDtypeStruct((t, wm.shape[1]), BF16),
                   jax.ShapeDtypeStruct((t, wa.shape[1]), BF16),
                   jax.ShapeDtypeStruct((t, LANES_V7X), F32),
                   jax.ShapeDtypeStruct((t, LANES_V7X), F32)],
        compiler_params=pltpu.CompilerParams(
            dimension_semantics=("arbitrary",), vmem_limit_bytes=VMEM_LIMIT_BYTES_V7X),
        name="pre_ffn_inproj",
    )(h, g1, wg3, wu3, wo3, gm, wm, wa, wgt)


def _post_kernel(h_ref, ym_ref, ya_ref, p_ref, wom_ref, woa_ref, g2_ref, wg_ref, wu_ref, wo_ref,
                 gp_ref, wpg_ref, wpp_ref, gf_ref, o_ref, *, final_norm):
    h = h_ref[...] + _dot(ym_ref[...], wom_ref[...]) + _dot(ya_ref[...], woa_ref[...])
    xn = _rmsnorm(h, g2_ref[...]).astype(BF16)
    h = h + 0.5 * _swiglu(xn, wg_ref, wu_ref, wo_ref)
    gate = jax.nn.sigmoid(_dot(_rmsnorm(h, gp_ref[...]).astype(BF16), wpg_ref[...]))
    h = h + gate * _dot(p_ref[...].astype(BF16), wpp_ref[...])
    if final_norm:
        h = _rmsnorm(h, gf_ref[...])
    o_ref[...] = h


def _post_call(h1, ym, ya, p, wom, woa, g2, wg3, wu3, wo3, gp, wpg, wpp, gf, final_norm):
    t, d = h1.shape
    tm = TOKEN_TILE
    row = lambda n: pl.BlockSpec((tm, n), lambda i: (i, 0))
    weights = (wom, woa, g2, wg3, wu3, wo3, gp, wpg, wpp, gf)
    return pl.pallas_call(
        functools.partial(_post_kernel, final_norm=final_norm),
        grid=(t // tm,),
        in_specs=[row(d), row(ym.shape[1]), row(ya.shape[1]), row(p.shape[1])]
                 + [_resident(w.shape) for w in weights],
        out_specs=row(d),
        out_shape=jax.ShapeDtypeStruct((t, d), F32),
        compiler_params=pltpu.CompilerParams(
            dimension_semantics=("arbitrary",), vmem_limit_bytes=VMEM_LIMIT_BYTES_V7X),
        name="post_outproj_ffn_ple",
    )(h1, ym, ya, p, *weights)


def _attn_kernel(lam_ref, slope_ref, g_ref, q_ref, k_ref, v_ref, o_ref, tab_s, vt_s,
                 *, lam_init, tq):
    s = k_ref.shape[1]
    nq = s // tq

    @pl.when(pl.program_id(1) == 0)
    def _():
        r = lax.broadcasted_iota(jnp.int32, tab_s.shape, 1)
        c = lax.broadcasted_iota(jnp.int32, tab_s.shape, 0)
        tab_s[...] = jnp.abs(r - c + (s - tq)).astype(F32) * slope_ref[0, :, 0:1]

    lam_v = lam_ref[...]
    lam = (jnp.exp(jnp.sum(lam_v[0:1] * lam_v[1:2], axis=1, keepdims=True))
           - jnp.exp(jnp.sum(lam_v[2:3] * lam_v[3:4], axis=1, keepdims=True)) + lam_init)

    k = k_ref[0]
    vt_s[...] = v_ref[0].astype(F32).T.astype(BF16)
    gain = g_ref[0] * (1.0 - lam_init)
    lane = lax.broadcasted_iota(jnp.int32, (tq, 2 * A_DK), 1)

    def softmax_pv(qc, bias):
        st = _dot_nt(k, qc) - bias
        m = jnp.max(st, axis=0, keepdims=True)
        e = jnp.exp2(st - m)
        l = jnp.sum(e, axis=0, keepdims=True)
        return _dot(vt_s[...], e.astype(BF16)) * (1.0 / l)

    def body(i, carry):
        q0 = pl.multiple_of(i * tq, tq)
        q = q_ref[0, pl.ds(q0, tq), :]
        zero = jnp.zeros_like(q)
        bias = tab_s[pl.ds(pl.multiple_of((nq - 1 - i) * tq, tq), s), :]
        o1 = softmax_pv(jnp.where(lane < A_DK, q, zero), bias)
        o2 = softmax_pv(jnp.where(lane >= A_DK, q, zero), bias)
        o = o1 - lam * o2
        on = o * lax.rsqrt(jnp.mean(o * o, axis=0, keepdims=True) + EPS)
        o_ref[0, pl.ds(q0, tq), :] = (on.T * gain).astype(o_ref.dtype)
        return carry

    lax.fori_loop(0, nq, body, 0)


def _attn_call(za, lam_vec, slopes, g_anorm, lam_init):
    b, s, _ = za.shape
    tq = ATTN_Q_TILE
    blk = 2 * A_DK
    assert blk == A_DV == LANES_V7X
    head_block = lambda off: pl.BlockSpec((1, s, blk), lambda h, i: (i, 0, off + h))
    return pl.pallas_call(
        functools.partial(_attn_kernel, lam_init=lam_init, tq=tq),
        grid=(A_HEADS, b),
        in_specs=[pl.BlockSpec(lam_vec.shape, lambda h, i: (0, 0)),
                  pl.BlockSpec((1, 1, LANES_V7X), lambda h, i: (h, 0, 0)),
                  pl.BlockSpec((1, 1, A_DV), lambda h, i: (h, 0, 0)),
                  head_block(0), head_block(A_HEADS), head_block(2 * A_HEADS)],
        out_specs=pl.BlockSpec((1, s, A_DV), lambda h, i: (i, 0, h)),
        out_shape=jax.ShapeDtypeStruct((b, s, A_HEADS * A_DV), BF16),
        scratch_shapes=[pltpu.VMEM((2 * s - tq, tq), F32), pltpu.VMEM((A_DV, s), BF16)],
        compiler_params=pltpu.CompilerParams(
            dimension_semantics=("arbitrary", "arbitrary"), vmem_limit_bytes=VMEM_LIMIT_BYTES_V7X),
        name="diff_attention",
    )(lam_vec, slopes, g_anorm, za, za, za)


def _log_sigmoid(x):
    return jnp.minimum(x, 0.0) - jnp.log(1.0 + jnp.exp(-jnp.abs(x)))


def _chunk_scan(x, op, fill, row_in_chunk, chunk, reverse):
    n = x.shape[0]
    k = 1
    while k < chunk:
        if reverse:
            shifted = pltpu.roll(x, n - k, 0)
            valid = row_in_chunk < chunk - k
        else:
            shifted = pltpu.roll(x, k, 0)
            valid = row_in_chunk >= k
        x = op(x, jnp.where(valid, shifted, fill))
        k *= 2
    return x


def _mlstm_kernel(zm_ref, gi_ref, gf_ref, convw_ref, bi_ref, bf_ref, gn_ref, o_ref,
                  qx_s, kt_s, b_s, cmax_s, crow_s, c_s, m_s, hf_s, hb_s, *, chunk):
    s = zm_ref.shape[1]
    nc = s // chunk
    nseq = 2 * M_HEADS
    dqk = M_HEADS * M_DK

    cw = convw_ref[...]
    lane_q = lax.broadcasted_iota(jnp.int32, (chunk, 2 * M_DK), 1)
    for j in range(nc):
        r0 = j * chunk
        halo = SUBLANES_V7X
        parts = []
        if j == 0:
            parts.append(jnp.zeros((halo, 2 * dqk), F32))
        else:
            parts.append(zm_ref[0, r0 - halo:r0, 0:2 * dqk].astype(F32))
        parts.append(zm_ref[0, r0:r0 + chunk, 0:2 * dqk].astype(F32))
        if j == nc - 1:
            parts.append(jnp.zeros((halo, 2 * dqk), F32))
        else:
            parts.append(zm_ref[0, r0 + chunk:r0 + chunk + halo, 0:2 * dqk].astype(F32))
        xe = jnp.concatenate(parts, axis=0)
        ne = chunk + 2 * halo
        y = (pltpu.roll(xe, 1, 0) * cw[0:1] + xe * cw[1:2] + pltpu.roll(xe, ne - 1, 0) * cw[2:3])
        y = y[halo:halo + chunk]
        y = y * jax.nn.sigmoid(y)
        q = y[:, :dqk].astype(BF16)
        kk = y[:, dqk:] * (M_DK ** -0.5)
        kt_s[:, r0:r0 + chunk] = kk.T
        for h in range(M_HEADS):
            pair = q[:, (h // 2) * 2 * M_DK:(h // 2 + 1) * 2 * M_DK]
            keep = (lane_q >= M_DK) if (h % 2) else (lane_q < M_DK)
            qx_s[r0:r0 + chunk, h * 2 * M_DK:(h + 1) * 2 * M_DK] = jnp.where(keep, pair, jnp.zeros_like(pair))

    li = (gi_ref[0] + bi_ref[...]) * LOG2E
    lf = _log_sigmoid(gf_ref[0] + bf_ref[...]) * LOG2E
    row_in_chunk = lax.broadcasted_iota(jnp.int32, li.shape, 0) % chunk
    is_fw = lax.broadcasted_iota(jnp.int32, li.shape, 1) < M_HEADS
    b = jnp.where(is_fw, _chunk_scan(lf, jnp.add, 0.0, row_in_chunk, chunk, False),
                  _chunk_scan(lf, jnp.add, 0.0, row_in_chunk, chunk, True))
    c = li - b
    cmax = jnp.where(is_fw, _chunk_scan(c, jnp.maximum, -jnp.inf, row_in_chunk, chunk, False),
                     _chunk_scan(c, jnp.maximum, -jnp.inf, row_in_chunk, chunk, True))
    b_s[...] = b
    cmax_s[...] = cmax
    for j in range(nc):
        crow_s[j] = c[j * chunk:(j + 1) * chunk, :].T[0:nseq, :]

    c_s[...] = jnp.zeros_like(c_s)
    m_s[...] = jnp.zeros_like(m_s)

    rr = lax.broadcasted_iota(jnp.int32, (chunk, chunk), 0)
    cc = lax.broadcasted_iota(jnp.int32, (chunk, chunk), 1)
    causal = (cc <= rr, cc >= rr)
    ones = jnp.ones((chunk, M_DV), BF16)

    def step(j, carry):
        for d in range(2):
            cj = j if d == 0 else nc - 1 - j
            r0 = pl.multiple_of(cj * chunk, chunk)
            rows = pl.ds(r0, chunk)
            last = chunk - 1 if d == 0 else 0
            h_out = hf_s if d == 0 else hb_s
            for h in range(M_HEADS):
                i = d * M_HEADS + h
                pr, half = h // 2, h % 2
                blk = slice(h * M_DV, (h + 1) * M_DV)
                qc = qx_s[rows, blk]
                ktp = kt_s[pr * 2 * M_DK:(pr + 1) * 2 * M_DK, rows]
                kth = kt_s[h * M_DK:(h + 1) * M_DK, rows]
                v_aug = jnp.concatenate([zm_ref[0, rows, 2 * dqk + h * M_DV:2 * dqk + (h + 1) * M_DV],
                                         ones], axis=1)
                bcol = b_s[rows, i:i + 1]
                mcol = jnp.maximum(m_s[i:i + 1, 0:1], cmax_s[rows, i:i + 1])
                crow = crow_s[cj, i:i + 1, :]
                m_old = m_s[i:i + 1, 0:1]
                dt = jnp.where(causal[d], jnp.exp2(crow - mcol), 0.0)
                sm = (_dot(qc, ktp.astype(BF16)) * dt).astype(BF16)
                state = c_s[d, pr]
                tot = jnp.exp2(m_old - mcol) * _dot(qc, state.astype(BF16)) + _dot(sm, v_aug)
                num = tot[:, :M_DV]
                den = tot[:, M_DV:]
                floor = jnp.exp2(-(bcol + mcol))
                h_out[rows, blk] = num / jnp.maximum(jnp.abs(den), floor)
                m_last = mcol[last:last + 1]
                wk = jnp.exp2(crow - m_last)
                dc = _dot((kth * wk).astype(BF16), v_aug)
                sub = slice(half * M_DK, (half + 1) * M_DK)
                c_s[d, pr, sub, :] = jnp.exp2(m_old - m_last) * state[sub] + dc
                m_s[i:i + 1, :] = jnp.broadcast_to(bcol[last:last + 1] + m_last, (1, m_s.shape[1]))
        return carry

    lax.fori_loop(0, nc, step, 0)

    gn = gn_ref[...]
    dv = M_HEADS * M_DV

    def finish(j, carry):
        rows = pl.ds(pl.multiple_of(j * chunk, chunk), chunk)
        for h in range(M_HEADS):
            blk = slice(h * M_DV, (h + 1) * M_DV)
            hh = hf_s[rows, blk] + hb_s[rows, blk]
            hn = hh * lax.rsqrt(jnp.mean(hh * hh, axis=-1, keepdims=True) + EPS) * gn[:, blk]
            og = zm_ref[0, rows, 2 * dqk + dv + h * M_DV:2 * dqk + dv + (h + 1) * M_DV].astype(F32)
            o_ref[0, rows, blk] = (jax.nn.sigmoid(og) * hn).astype(o_ref.dtype)
        return carry

    lax.fori_loop(0, nc, finish, 0)


def _mlstm_call(zm, zgi, zgf, conv_w, bi, bf, g_mnorm):
    b, s, wz = zm.shape
    chunk = MLSTM_CHUNK
    nc = s // chunk
    dv = M_HEADS * M_DV
    dqk = M_HEADS * M_DK
    full = lambda a: pl.BlockSpec(a.shape, lambda i: (0,) * a.ndim)
    per_b = lambda n: pl.BlockSpec((1, s, n), lambda i: (i, 0, 0))
    return pl.pallas_call(
        functools.partial(_mlstm_kernel, chunk=chunk),
        grid=(b,),
        in_specs=[per_b(wz), per_b(LANES_V7X), per_b(LANES_V7X), full(conv_w), full(bi), full(bf),
                  full(g_mnorm)],
        out_specs=per_b(dv),
        out_shape=jax.ShapeDtypeStruct((b, s, dv), BF16),
        scratch_shapes=[
            pltpu.VMEM((s, M_HEADS * 2 * M_DK), BF16),
            pltpu.VMEM((dqk, s), F32),
            pltpu.VMEM((s, LANES_V7X), F32),
            pltpu.VMEM((s, LANES_V7X), F32),
            pltpu.VMEM((nc, 2 * M_HEADS, chunk), F32),
            pltpu.VMEM((2, M_HEADS // 2, 2 * M_DK, 2 * M_DV), F32),
            pltpu.VMEM((2 * M_HEADS, LANES_V7X), F32),
            pltpu.VMEM((s, dv), F32),
            pltpu.VMEM((s, dv), F32),
        ],
        compiler_params=pltpu.CompilerParams(
            dimension_semantics=("arbitrary",), vmem_limit_bytes=VMEM_LIMIT_BYTES_V7X),
        name="mlstm_bidir",
    )(zm, zgi, zgf, conv_w, bi, bf, g_mnorm)


def _chunk_ffn_weights(w_in, w_out):
    d, f2 = w_in.shape
    f = f2 // 2
    nf = f // FF_CHUNK
    wg3 = w_in[:, :f].reshape(d, nf, FF_CHUNK).transpose(1, 0, 2).astype(BF16)
    wu3 = w_in[:, f:].reshape(d, nf, FF_CHUNK).transpose(1, 0, 2).astype(BF16)
    wo3 = w_out.reshape(nf, FF_CHUNK, d).astype(BF16)
    return wg3, wu3, wo3


def _gate_columns(a16):
    hh = M_HEADS
    pad = jnp.zeros(a16.shape[:-1] + (LANES_V7X - 2 * hh,), a16.dtype)
    gi = jnp.concatenate([a16[..., 0:hh], a16[..., 2 * hh:3 * hh], pad], axis=-1)
    gf = jnp.concatenate([a16[..., hh:2 * hh], a16[..., 3 * hh:4 * hh], pad], axis=-1)
    return gi, gf


def kernel(x, p, g_ffn1, w_ffn1_in, w_ffn1_out, g_mix, w_in, b_mgate, conv_w, g_mnorm, lam_q1, lam_k1, lam_q2, lam_k2, g_anorm, w_out, g_ffn2, w_ffn2_in, w_ffn2_out, g_ple, w_ple_gate, w_ple_proj, g_final):
    bsz, s, d = x.shape
    depth = w_in.shape[0]
    t = bsz * s
    m_qk, m_v, m_g = M_HEADS * M_DK, M_HEADS * M_DV, 4 * M_HEADS
    a_qk = A_HEADS * 2 * A_DK
    m_end = 2 * m_qk + 2 * m_v
    slopes = np.array([2.0 ** (-8.0 * (h + 1) / A_HEADS) for h in range(A_HEADS)], dtype=np.float32)
    slopes = jnp.broadcast_to(jnp.asarray(slopes * np.float32(LOG2E))[:, None, None],
                              (A_HEADS, 1, LANES_V7X))
    row = lambda v: v.reshape(1, -1).astype(F32)

    h = x.reshape(t, d)
    for i in range(depth):
        lam_init = 0.8 - 0.6 * math.exp(-0.3 * i)
        wg1, wu1, wo1 = _chunk_ffn_weights(w_ffn1_in[i], w_ffn1_out[i])
        wg2, wu2, wo2 = _chunk_ffn_weights(w_ffn2_in[i], w_ffn2_out[i])
        wi = w_in[i]
        wm = wi[:, :m_end].astype(BF16)
        wa = wi[:, m_end + m_g:]
        wa = jnp.concatenate([wa[:, :a_qk] * (A_DK ** -0.5 * LOG2E), wa[:, a_qk:]], axis=1).astype(BF16)
        wgi, wgf = _gate_columns(wi[:, m_end:m_end + m_g])
        wgt = jnp.concatenate([wgi, wgf], axis=1).astype(BF16)
        bi, bf = _gate_columns(b_mgate[i].reshape(1, m_g).astype(F32))

        h1, zm, za, zgi, zgf = _pre_call(h, row(g_ffn1[i]), wg1, wu1, wo1, row(g_mix[i]), wm, wa, wgt)
        ym = _mlstm_call(zm.reshape(bsz, s, -1), zgi.reshape(bsz, s, -1), zgf.reshape(bsz, s, -1),
                         conv_w[i].astype(F32), bi, bf, row(g_mnorm[i]))
        lam_vec = jnp.stack([lam_q1[i], lam_k1[i], lam_q2[i], lam_k2[i]]).astype(F32)
        ya = _attn_call(za.reshape(bsz, s, -1), lam_vec, slopes,
                        g_anorm[i].reshape(A_HEADS, 1, A_DV).astype(F32), lam_init)
        wo = w_out[i].astype(BF16)
        h = _post_call(h1, ym.reshape(t, -1), ya.reshape(t, -1), p[i].reshape(t, -1),
                       wo[:m_v], wo[m_v:], row(g_ffn2[i]), wg2, wu2, wo2, row(g_ple[i]),
                       w_ple_gate[i].astype(BF16), w_ple_proj[i].astype(BF16), row(g_final),
                       final_norm=(i == depth - 1))
    return h.reshape(bsz, s, d)
```

```python
import functools
import math

import jax
import jax.numpy as jnp
import numpy as np
from jax import lax
from jax.experimental import pallas as pl
from jax.experimental.pallas import tpu as pltpu

F32 = jnp.float32
BF16 = jnp.bfloat16

EPS = 1e-6
LOG2E = 1.4426950408889634

M_HEADS = 4
M_DK = 64
M_DV = 128
A_HEADS = 4
A_DK = 64
A_DV = 128
CONV_K = 3

LANES_V7X = 128
SUBLANES_V7X = 8
VMEM_LIMIT_BYTES_V7X = 56 * 1024 * 1024

TOKEN_TILE = 512
FF_CHUNK = 256
MLSTM_CHUNK = 256
ATTN_Q_TILE = 256
ATTN_KEY_BLOCK = 256


def _aligned(x, m):
    return x if isinstance(x, int) else pl.multiple_of(x, m)


def _dot(a, b):
    return jnp.dot(a, b, preferred_element_type=F32)


def _dot_nt(a, b):
    return lax.dot_general(a, b, (((1,), (1,)), ((), ())), preferred_element_type=F32)


def _rmsnorm(x, g):
    ms = jnp.mean(x * x, axis=-1, keepdims=True)
    return x * lax.rsqrt(ms + EPS) * g


def _swiglu(xn, wg_ref, wu_ref, wo_ref):
    acc = None
    for j in range(wg_ref.shape[0]):
        g = _dot(xn, wg_ref[j])
        u = _dot(xn, wu_ref[j])
        a = (g * jax.nn.sigmoid(g) * u).astype(BF16)
        d = _dot(a, wo_ref[j])
        acc = d if acc is None else acc + d
    return acc


def _pre_kernel(x_ref, g1_ref, wg_ref, wu_ref, wo_ref, gm_ref, wm_ref, wa_ref, wgt_ref,
                h_ref, zm_ref, za_ref, zgi_ref, zgf_ref):
    x = x_ref[...]
    xn = _rmsnorm(x, g1_ref[...]).astype(BF16)
    h = x + 0.5 * _swiglu(xn, wg_ref, wu_ref, wo_ref)
    h_ref[...] = h
    u = _rmsnorm(h, gm_ref[...]).astype(BF16)
    zm_ref[...] = _dot(u, wm_ref[...]).astype(BF16)
    za_ref[...] = _dot(u, wa_ref[...]).astype(BF16)
    zg = _dot(u, wgt_ref[...])
    zgi_ref[...] = zg[:, :LANES_V7X]
    zgf_ref[...] = zg[:, LANES_V7X:]


def _resident(shape):
    nd = len(shape)
    return pl.BlockSpec(shape, lambda *_: (0,) * nd, pipeline_mode=pl.Buffered(1))


def _pre_call(h, g1, wg3, wu3, wo3, gm, wm, wa, wgt):
    t, d = h.shape
    tm = TOKEN_TILE
    row = lambda n: pl.BlockSpec((tm, n), lambda i: (i, 0))
    return pl.pallas_call(
        _pre_kernel,
        grid=(t // tm,),
        in_specs=[row(d), _resident(g1.shape), _resident(wg3.shape), _resident(wu3.shape),
                  _resident(wo3.shape), _resident(gm.shape), _resident(wm.shape),
                  _resident(wa.shape), _resident(wgt.shape)],
        out_specs=[row(d), row(wm.shape[1]), row(wa.shape[1]), row(LANES_V7X), row(LANES_V7X)],
        out_shape=[jax.ShapeDtypeStruct((t, d), F32),
                   jax.ShapeDtypeStruct((t, wm.shape[1]), BF16),
                   jax.ShapeDtypeStruct((t, wa.shape[1]), BF16),
                   jax.ShapeDtypeStruct((t, LANES_V7X), F32),
                   jax.ShapeDtypeStruct((t, LANES_V7X), F32)],
        compiler_params=pltpu.CompilerParams(
            dimension_semantics=("arbitrary",), vmem_limit_bytes=VMEM_LIMIT_BYTES_V7X),
        name="pre_ffn_inproj",
    )(h, g1, wg3, wu3, wo3, gm, wm, wa, wgt)


def _post_kernel(h_ref, ym_ref, ya_ref, p_ref, wom_ref, woa_ref, g2_ref, wg_ref, wu_ref, wo_ref,
                 gp_ref, wpg_ref, wpp_ref, gf_ref, o_ref, *, final_norm):
    h = h_ref[...] + _dot(ym_ref[...], wom_ref[...]) + _dot(ya_ref[...], woa_ref[...])
    xn = _rmsnorm(h, g2_ref[...]).astype(BF16)
    h = h + 0.5 * _swiglu(xn, wg_ref, wu_ref, wo_ref)
    gate = jax.nn.sigmoid(_dot(_rmsnorm(h, gp_ref[...]).astype(BF16), wpg_ref[...]))
    h = h + gate * _dot(p_ref[...].astype(BF16), wpp_ref[...])
    if final_norm:
        h = _rmsnorm(h, gf_ref[...])
    o_ref[...] = h


def _post_call(h1, ym, ya, p, wom, woa, g2, wg3, wu3, wo3, gp, wpg, wpp, gf, final_norm):
    t, d = h1.shape
    tm = TOKEN_TILE
    row = lambda n: pl.BlockSpec((tm, n), lambda i: (i, 0))
    weights = (wom, woa, g2, wg3, wu3, wo3, gp, wpg, wpp, gf)
    return pl.pallas_call(
        functools.partial(_post_kernel, final_norm=final_norm),
        grid=(t // tm,),
        in_specs=[row(d), row(ym.shape[1]), row(ya.shape[1]), row(p.shape[1])]
                 + [_resident(w.shape) for w in weights],
        out_specs=row(d),
        out_shape=jax.ShapeDtypeStruct((t, d), F32),
        compiler_params=pltpu.CompilerParams(
            dimension_semantics=("arbitrary",), vmem_limit_bytes=VMEM_LIMIT_BYTES_V7X),
        name="post_outproj_ffn_ple",
    )(h1, ym, ya, p, *weights)


def _attn_kernel(lam_ref, slope_ref, g_ref, q_ref, k_ref, v_ref, o_ref, tab_s, vt_s, st0_s, st1_s,
                 *, lam_init, tq, kblk):
    s = k_ref.shape[1]
    nq = s // tq
    nkb = s // kblk

    @pl.when(pl.program_id(1) == 0)
    def _():
        r = lax.broadcasted_iota(jnp.int32, tab_s.shape, 1)
        c = lax.broadcasted_iota(jnp.int32, tab_s.shape, 0)
        tab_s[...] = jnp.abs(r - c + (s - tq)).astype(F32) * slope_ref[0, :, 0:1]

    lam_v = lam_ref[...]
    lam = (jnp.exp(jnp.sum(lam_v[0:1] * lam_v[1:2], axis=1, keepdims=True))
           - jnp.exp(jnp.sum(lam_v[2:3] * lam_v[3:4], axis=1, keepdims=True)) + lam_init)

    vt_s[...] = v_ref[0].astype(F32).T.astype(BF16)
    gain = g_ref[0] * (1.0 - lam_init)
    lane = lax.broadcasted_iota(jnp.int32, (tq, 2 * A_DK), 1)

    def masked_q(u):
        q = q_ref[0, pl.ds(_aligned(u * tq, tq), tq), :]
        zero = jnp.zeros_like(q)
        return jnp.concatenate([jnp.where(lane < A_DK, q, zero), jnp.where(lane >= A_DK, q, zero)], axis=0)

    def score_block(u, qm, kb, mx, dst):
        rows = slice(kb * kblk, (kb + 1) * kblk)
        bias = tab_s[pl.ds(_aligned((nq - 1 - u) * tq + kb * kblk, kblk), kblk), :]
        sc = _dot_nt(k_ref[0, rows, :], qm) - jnp.concatenate([bias, bias], axis=1)
        dst[rows, :] = sc
        return jnp.maximum(mx, jnp.max(sc, axis=0, keepdims=True))

    def pv_block(m, kb, l, acc, src):
        rows = slice(kb * kblk, (kb + 1) * kblk)
        e = jnp.exp2(src[rows, :] - m)
        return l + jnp.sum(e, axis=0, keepdims=True), acc + _dot(vt_s[:, rows], e.astype(BF16))

    def finish(u, l, acc):
        r = 1.0 / l
        o = acc[:, :tq] * r[:, :tq] - lam * (acc[:, tq:] * r[:, tq:])
        on = o * lax.rsqrt(jnp.mean(o * o, axis=0, keepdims=True) + EPS)
        o_ref[0, pl.ds(_aligned(u * tq, tq), tq), :] = (on.T * gain).astype(o_ref.dtype)

    neg = jnp.full((1, 2 * tq), -jnp.inf, F32)
    l0 = jnp.zeros((1, 2 * tq), F32)
    acc0 = jnp.zeros((A_DV, 2 * tq), F32)

    def stage(u, m_prev, dst, src):
        mx, l, acc = neg, l0, acc0
        qm = masked_q(u) if dst is not None else None
        for kb in range(nkb):
            if dst is not None:
                mx = score_block(u, qm, kb, mx, dst)
            if src is not None:
                l, acc = pv_block(m_prev, kb, l, acc, src)
        if src is not None:
            finish(u - 1, l, acc)
        return mx

    assert nq % 2 == 0
    m0 = stage(0, None, st0_s, None)

    def body(j, m_even):
        m_odd = stage(2 * j + 1, m_even, st1_s, st0_s)
        return stage(2 * j + 2, m_odd, st0_s, st1_s)

    m_even = lax.fori_loop(0, nq // 2 - 1, body, m0)
    m_odd = stage(nq - 1, m_even, st1_s, st0_s)
    stage(nq, m_odd, None, st1_s)


def _attn_call(za, lam_vec, slopes, g_anorm, lam_init):
    b, s, _ = za.shape
    tq = ATTN_Q_TILE
    blk = 2 * A_DK
    assert blk == A_DV == LANES_V7X
    head_block = lambda off: pl.BlockSpec((1, s, blk), lambda h, i: (i, 0, off + h))
    return pl.pallas_call(
        functools.partial(_attn_kernel, lam_init=lam_init, tq=tq, kblk=ATTN_KEY_BLOCK),
        grid=(A_HEADS, b),
        in_specs=[pl.BlockSpec(lam_vec.shape, lambda h, i: (0, 0)),
                  pl.BlockSpec((1, 1, LANES_V7X), lambda h, i: (h, 0, 0)),
                  pl.BlockSpec((1, 1, A_DV), lambda h, i: (h, 0, 0)),
                  head_block(0), head_block(A_HEADS), head_block(2 * A_HEADS)],
        out_specs=pl.BlockSpec((1, s, A_DV), lambda h, i: (i, 0, h)),
        out_shape=jax.ShapeDtypeStruct((b, s, A_HEADS * A_DV), BF16),
        scratch_shapes=[pltpu.VMEM((2 * s - tq, tq), F32),
                        pltpu.VMEM((A_DV, s), BF16),
                        pltpu.VMEM((s, 2 * tq), F32),
                        pltpu.VMEM((s, 2 * tq), F32)],
        compiler_params=pltpu.CompilerParams(
            dimension_semantics=("arbitrary", "arbitrary"), vmem_limit_bytes=VMEM_LIMIT_BYTES_V7X),
        name="diff_attention",
    )(lam_vec, slopes, g_anorm, za, za, za)


def _log_sigmoid(x):
    return jnp.minimum(x, 0.0) - jnp.log(1.0 + jnp.exp(-jnp.abs(x)))


def _chunk_scan(x, op, fill, row_in_chunk, chunk, reverse):
    n = x.shape[0]
    k = 1
    while k < chunk:
        if reverse:
            shifted = pltpu.roll(x, n - k, 0)
            valid = row_in_chunk < chunk - k
        else:
            shifted = pltpu.roll(x, k, 0)
            valid = row_in_chunk >= k
        x = op(x, jnp.where(valid, shifted, fill))
        k *= 2
    return x


def _mlstm_kernel(zm_ref, gi_ref, gf_ref, convw_ref, bi_ref, bf_ref, gn_ref, o_ref,
                  qx_s, kt_s, b_s, cmax_s, crow_s, c_s, m_s, hf_s, hb_s, *, chunk):
    s = zm_ref.shape[1]
    nc = s // chunk
    nseq = 2 * M_HEADS
    dqk = M_HEADS * M_DK

    cw = convw_ref[...]
    lane_q = lax.broadcasted_iota(jnp.int32, (chunk, 2 * M_DK), 1)
    for j in range(nc):
        r0 = j * chunk
        halo = SUBLANES_V7X
        parts = []
        if j == 0:
            parts.append(jnp.zeros((halo, 2 * dqk), F32))
        else:
            parts.append(zm_ref[0, r0 - halo:r0, 0:2 * dqk].astype(F32))
        parts.append(zm_ref[0, r0:r0 + chunk, 0:2 * dqk].astype(F32))
        if j == nc - 1:
            parts.append(jnp.zeros((halo, 2 * dqk), F32))
        else:
            parts.append(zm_ref[0, r0 + chunk:r0 + chunk + halo, 0:2 * dqk].astype(F32))
        xe = jnp.concatenate(parts, axis=0)
        ne = chunk + 2 * halo
        y = (pltpu.roll(xe, 1, 0) * cw[0:1] + xe * cw[1:2] + pltpu.roll(xe, ne - 1, 0) * cw[2:3])
        y = y[halo:halo + chunk]
        y = y * jax.nn.sigmoid(y)
        q = y[:, :dqk].astype(BF16)
        kk = y[:, dqk:] * (M_DK ** -0.5)
        kt_s[:, r0:r0 + chunk] = kk.T
        for h in range(M_HEADS):
            pair = q[:, (h // 2) * 2 * M_DK:(h // 2 + 1) * 2 * M_DK]
            keep = (lane_q >= M_DK) if (h % 2) else (lane_q < M_DK)
            qx_s[r0:r0 + chunk, h * 2 * M_DK:(h + 1) * 2 * M_DK] = jnp.where(keep, pair, jnp.zeros_like(pair))

    li = (gi_ref[0] + bi_ref[...]) * LOG2E
    lf = _log_sigmoid(gf_ref[0] + bf_ref[...]) * LOG2E
    row_in_chunk = lax.broadcasted_iota(jnp.int32, li.shape, 0) % chunk
    is_fw = lax.broadcasted_iota(jnp.int32, li.shape, 1) < M_HEADS
    b = jnp.where(is_fw, _chunk_scan(lf, jnp.add, 0.0, row_in_chunk, chunk, False),
                  _chunk_scan(lf, jnp.add, 0.0, row_in_chunk, chunk, True))
    c = li - b
    cmax = jnp.where(is_fw, _chunk_scan(c, jnp.maximum, -jnp.inf, row_in_chunk, chunk, False),
                     _chunk_scan(c, jnp.maximum, -jnp.inf, row_in_chunk, chunk, True))
    b_s[...] = b
    cmax_s[...] = cmax
    for j in range(nc):
        crow_s[j] = c[j * chunk:(j + 1) * chunk, :].T[0:nseq, :]

    c_s[...] = jnp.zeros_like(c_s)
    m_s[...] = jnp.zeros_like(m_s)

    rr = lax.broadcasted_iota(jnp.int32, (chunk, chunk), 0)
    cc = lax.broadcasted_iota(jnp.int32, (chunk, chunk), 1)
    causal = (cc <= rr, cc >= rr)
    ones = jnp.ones((chunk, M_DV), BF16)

    def step(j, carry):
        for d in range(2):
            cj = j if d == 0 else nc - 1 - j
            r0 = pl.multiple_of(cj * chunk, chunk)
            rows = pl.ds(r0, chunk)
            last = chunk - 1 if d == 0 else 0
            h_out = hf_s if d == 0 else hb_s
            for h in range(M_HEADS):
                i = d * M_HEADS + h
                pr, half = h // 2, h % 2
                blk = slice(h * M_DV, (h + 1) * M_DV)
                qc = qx_s[rows, blk]
                ktp = kt_s[pr * 2 * M_DK:(pr + 1) * 2 * M_DK, rows]
                kth = kt_s[h * M_DK:(h + 1) * M_DK, rows]
                v_aug = jnp.concatenate([zm_ref[0, rows, 2 * dqk + h * M_DV:2 * dqk + (h + 1) * M_DV],
                                         ones], axis=1)
                bcol = b_s[rows, i:i + 1]
                mcol = jnp.maximum(m_s[i:i + 1, 0:1], cmax_s[rows, i:i + 1])
                crow = crow_s[cj, i:i + 1, :]
                m_old = m_s[i:i + 1, 0:1]
                dt = jnp.where(causal[d], jnp.exp2(crow - mcol), 0.0)
                sm = (_dot(qc, ktp.astype(BF16)) * dt).astype(BF16)
                state = c_s[d, pr]
                tot = jnp.exp2(m_old - mcol) * _dot(qc, state.astype(BF16)) + _dot(sm, v_aug)
                num = tot[:, :M_DV]
                den = tot[:, M_DV:]
                floor = jnp.exp2(-(bcol + mcol))
                h_out[rows, blk] = num / jnp.maximum(jnp.abs(den), floor)
                m_last = mcol[last:last + 1]
                wk = jnp.exp2(crow - m_last)
                dc = _dot((kth * wk).astype(BF16), v_aug)
                sub = slice(half * M_DK, (half + 1) * M_DK)
                c_s[d, pr, sub, :] = jnp.exp2(m_old - m_last) * state[sub] + dc
                m_s[i:i + 1, :] = jnp.broadcast_to(bcol[last:last + 1] + m_last, (1, m_s.shape[1]))
        return carry

    lax.fori_loop(0, nc, step, 0)

    gn = gn_ref[...]
    dv = M_HEADS * M_DV

    def finish(j, carry):
        rows = pl.ds(pl.multiple_of(j * chunk, chunk), chunk)
        for h in range(M_HEADS):
            blk = slice(h * M_DV, (h + 1) * M_DV)
            hh = hf_s[rows, blk] + hb_s[rows, blk]
            hn = hh * lax.rsqrt(jnp.mean(hh * hh, axis=-1, keepdims=True) + EPS) * gn[:, blk]
            og = zm_ref[0, rows, 2 * dqk + dv + h * M_DV:2 * dqk + dv + (h + 1) * M_DV].astype(F32)
            o_ref[0, rows, blk] = (jax.nn.sigmoid(og) * hn).astype(o_ref.dtype)
        return carry

    lax.fori_loop(0, nc, finish, 0)


def _mlstm_call(zm, zgi, zgf, conv_w, bi, bf, g_mnorm):
    b, s, wz = zm.shape
    chunk = MLSTM_CHUNK
    nc = s // chunk
    dv = M_HEADS * M_DV
    dqk = M_HEADS * M_DK
    full = lambda a: pl.BlockSpec(a.shape, lambda i: (0,) * a.ndim)
    per_b = lambda n: pl.BlockSpec((1, s, n), lambda i: (i, 0, 0))
    return pl.pallas_call(
        functools.partial(_mlstm_kernel, chunk=chunk),
        grid=(b,),
        in_specs=[per_b(wz), per_b(LANES_V7X), per_b(LANES_V7X), full(conv_w), full(bi), full(bf),
                  full(g_mnorm)],
        out_specs=per_b(dv),
        out_shape=jax.ShapeDtypeStruct((b, s, dv), BF16),
        scratch_shapes=[
            pltpu.VMEM((s, M_HEADS * 2 * M_DK), BF16),
            pltpu.VMEM((dqk, s), F32),
            pltpu.VMEM((s, LANES_V7X), F32),
            pltpu.VMEM((s, LANES_V7X), F32),
            pltpu.VMEM((nc, 2 * M_HEADS, chunk), F32),
            pltpu.VMEM((2, M_HEADS // 2, 2 * M_DK, 2 * M_DV), F32),
            pltpu.VMEM((2 * M_HEADS, LANES_V7X), F32),
            pltpu.VMEM((s, dv), F32),
            pltpu.VMEM((s, dv), F32),
        ],
        compiler_params=pltpu.CompilerParams(
            dimension_semantics=("arbitrary",), vmem_limit_bytes=VMEM_LIMIT_BYTES_V7X),
        name="mlstm_bidir",
    )(zm, zgi, zgf, conv_w, bi, bf, g_mnorm)


def _chunk_ffn_weights(w_in, w_out):
    d, f2 = w_in.shape
    f = f2 // 2
    nf = f // FF_CHUNK
    wg3 = w_in[:, :f].reshape(d, nf, FF_CHUNK).transpose(1, 0, 2).astype(BF16)
    wu3 = w_in[:, f:].reshape(d, nf, FF_CHUNK).transpose(1, 0, 2).astype(BF16)
    wo3 = w_out.reshape(nf, FF_CHUNK, d).astype(BF16)
    return wg3, wu3, wo3


def _gate_columns(a16):
    hh = M_HEADS
    pad = jnp.zeros(a16.shape[:-1] + (LANES_V7X - 2 * hh,), a16.dtype)
    gi = jnp.concatenate([a16[..., 0:hh], a16[..., 2 * hh:3 * hh], pad], axis=-1)
    gf = jnp.concatenate([a16[..., hh:2 * hh], a16[..., 3 * hh:4 * hh], pad], axis=-1)
    return gi, gf


def kernel(x, p, g_ffn1, w_ffn1_in, w_ffn1_out, g_mix, w_in, b_mgate, conv_w, g_mnorm, lam_q1, lam_k1, lam_q2, lam_k2, g_anorm, w_out, g_ffn2, w_ffn2_in, w_ffn2_out, g_ple, w_ple_gate, w_ple_proj, g_final):
    bsz, s, d = x.shape
    depth = w_in.shape[0]
    t = bsz * s
    m_qk, m_v, m_g = M_HEADS * M_DK, M_HEADS * M_DV, 4 * M_HEADS
    a_qk = A_HEADS * 2 * A_DK
    m_end = 2 * m_qk + 2 * m_v
    slopes = np.array([2.0 ** (-8.0 * (h + 1) / A_HEADS) for h in range(A_HEADS)], dtype=np.float32)
    slopes = jnp.broadcast_to(jnp.asarray(slopes * np.float32(LOG2E))[:, None, None],
                              (A_HEADS, 1, LANES_V7X))
    row = lambda v: v.reshape(1, -1).astype(F32)

    h = x.reshape(t, d)
    for i in range(depth):
        lam_init = 0.8 - 0.6 * math.exp(-0.3 * i)
        wg1, wu1, wo1 = _chunk_ffn_weights(w_ffn1_in[i], w_ffn1_out[i])
        wg2, wu2, wo2 = _chunk_ffn_weights(w_ffn2_in[i], w_ffn2_out[i])
        wi = w_in[i]
        wm = wi[:, :m_end].astype(BF16)
        wa = wi[:, m_end + m_g:]
        wa = jnp.concatenate([wa[:, :a_qk] * (A_DK ** -0.5 * LOG2E), wa[:, a_qk:]], axis=1).astype(BF16)
        wgi, wgf = _gate_columns(wi[:, m_end:m_end + m_g])
        wgt = jnp.concatenate([wgi, wgf], axis=1).astype(BF16)
        bi, bf = _gate_columns(b_mgate[i].reshape(1, m_g).astype(F32))

        h1, zm, za, zgi, zgf = _pre_call(h, row(g_ffn1[i]), wg1, wu1, wo1, row(g_mix[i]), wm, wa, wgt)
        ym = _mlstm_call(zm.reshape(bsz, s, -1), zgi.reshape(bsz, s, -1), zgf.reshape(bsz, s, -1),
                         conv_w[i].astype(F32), bi, bf, row(g_mnorm[i]))
        lam_vec = jnp.stack([lam_q1[i], lam_k1[i], lam_q2[i], lam_k2[i]]).astype(F32)
        ya = _attn_call(za.reshape(bsz, s, -1), lam_vec, slopes,
                        g_anorm[i].reshape(A_HEADS, 1, A_DV).astype(F32), lam_init)
        wo = w_out[i].astype(BF16)
        h = _post_call(h1, ym.reshape(t, -1), ya.reshape(t, -1), p[i].reshape(t, -1),
                       wo[:m_v], wo[m_v:], row(g_ffn2[i]), wg2, wu2, wo2, row(g_ple[i]),
                       w_ple_gate[i].astype(BF16), w_ple_proj[i].astype(BF16), row(g_final),
                       final_norm=(i == depth - 1))
    return h.reshape(bsz, s, d)
```

```python
import functools
import math

import jax
import jax.numpy as jnp
import numpy as np
from jax import lax
from jax.experimental import pallas as pl
from jax.experimental.pallas import tpu as pltpu

F32 = jnp.float32
BF16 = jnp.bfloat16

EPS = 1e-6
LOG2E = 1.4426950408889634

M_HEADS = 4
M_DK = 64
M_DV = 128
A_HEADS = 4
A_DK = 64
A_DV = 128
CONV_K = 3

LANES_V7X = 128
SUBLANES_V7X = 8
BF16_ROWS_PER_VREG_V7X = 16
VMEM_LIMIT_BYTES_V7X = 56 * 1024 * 1024

TOKEN_TILE = 512
FF_CHUNK = 256
MLSTM_CHUNK = 128
MLSTM_CHUNKS_PER_STEP = 4
MLSTM_ROW_BLOCK = 256
ATTN_Q_TILE = 256
ATTN_KEY_BLOCK = 256


def _aligned(x, m):
    return x if isinstance(x, int) else pl.multiple_of(x, m)


def _dot(a, b):
    return jnp.dot(a, b, preferred_element_type=F32)


def _dot_nt(a, b):
    return lax.dot_general(a, b, (((1,), (1,)), ((), ())), preferred_element_type=F32)


def _rmsnorm(x, g):
    ms = jnp.mean(x * x, axis=-1, keepdims=True)
    return x * lax.rsqrt(ms + EPS) * g


def _swiglu(xn, win_ref, wo_ref):
    f = wo_ref.shape[0]
    acc = None
    for c0 in range(0, f, FF_CHUNK):
        g = _dot(xn, win_ref[:, c0:c0 + FF_CHUNK])
        u = _dot(xn, win_ref[:, f + c0:f + c0 + FF_CHUNK])
        a = (g * jax.nn.sigmoid(g) * u).astype(BF16)
        d = _dot(a, wo_ref[c0:c0 + FF_CHUNK, :])
        acc = d if acc is None else acc + d
    return acc


def _resident(shape):
    nd = len(shape)
    return pl.BlockSpec(shape, lambda *_: (0,) * nd, pipeline_mode=pl.Buffered(1))


def _pre_kernel(x_ref, g1_ref, win_ref, wo_ref, gm_ref, wm_ref, wa_ref, wgt_ref,
                h_ref, zm_ref, za_ref, zgt_ref):
    x = x_ref[...]
    xn = _rmsnorm(x, g1_ref[...]).astype(BF16)
    h = x + 0.5 * _swiglu(xn, win_ref, wo_ref)
    h_ref[...] = h
    u = _rmsnorm(h, gm_ref[...]).astype(BF16)
    zm_ref[...] = _dot(u, wm_ref[...]).astype(BF16)
    za_ref[...] = _dot(u, wa_ref[...]).astype(BF16)
    zgt_ref[...] = _dot_nt(wgt_ref[...], u)


def _pre_call(h, g1, win, wo, gm, wm, wa, wgt):
    t, d = h.shape
    tm = TOKEN_TILE
    ng = wgt.shape[0]
    row = lambda n: pl.BlockSpec((tm, n), lambda i: (i, 0))
    weights = (g1, win, wo, gm, wm, wa, wgt)
    return pl.pallas_call(
        _pre_kernel,
        grid=(t // tm,),
        in_specs=[row(d)] + [_resident(w.shape) for w in weights],
        out_specs=[row(d), row(wm.shape[1]), row(wa.shape[1]), pl.BlockSpec((ng, tm), lambda i: (0, i))],
        out_shape=[jax.ShapeDtypeStruct((t, d), F32),
                   jax.ShapeDtypeStruct((t, wm.shape[1]), BF16),
                   jax.ShapeDtypeStruct((t, wa.shape[1]), BF16),
                   jax.ShapeDtypeStruct((ng, t), F32)],
        compiler_params=pltpu.CompilerParams(
            dimension_semantics=("arbitrary",), vmem_limit_bytes=VMEM_LIMIT_BYTES_V7X),
        name="pre_ffn_inproj",
    )(h, *weights)


def _post_kernel(h_ref, ym_ref, ya_ref, p_ref, wom_ref, woa_ref, g2_ref, win_ref, wo_ref,
                 gp_ref, wpg_ref, wpp_ref, gf_ref, o_ref, *, final_norm):
    h = h_ref[...] + _dot(ym_ref[...], wom_ref[...]) + _dot(ya_ref[...], woa_ref[...])
    xn = _rmsnorm(h, g2_ref[...]).astype(BF16)
    h = h + 0.5 * _swiglu(xn, win_ref, wo_ref)
    gate = jax.nn.sigmoid(_dot(_rmsnorm(h, gp_ref[...]).astype(BF16), wpg_ref[...]))
    h = h + gate * _dot(p_ref[...].astype(BF16), wpp_ref[...])
    if final_norm:
        h = _rmsnorm(h, gf_ref[...])
    o_ref[...] = h


def _post_call(h1, ym, ya, p, wom, woa, g2, win, wo, gp, wpg, wpp, gf, final_norm):
    t, d = h1.shape
    tm = TOKEN_TILE
    row = lambda n: pl.BlockSpec((tm, n), lambda i: (i, 0))
    weights = (wom, woa, g2, win, wo, gp, wpg, wpp, gf)
    return pl.pallas_call(
        functools.partial(_post_kernel, final_norm=final_norm),
        grid=(t // tm,),
        in_specs=[row(d), row(ym.shape[1]), row(ya.shape[1]), row(p.shape[1])]
                 + [_resident(w.shape) for w in weights],
        out_specs=row(d),
        out_shape=jax.ShapeDtypeStruct((t, d), F32),
        compiler_params=pltpu.CompilerParams(
            dimension_semantics=("arbitrary",), vmem_limit_bytes=VMEM_LIMIT_BYTES_V7X),
        name="post_outproj_ffn_ple",
    )(h1, ym, ya, p, *weights)


def _attn_kernel(lam_ref, slope_ref, g_ref, q_ref, k_ref, v_ref, o_ref, tab_s, vt_s, st0_s, st1_s,
                 *, lam_init, tq, kblk):
    s = k_ref.shape[1]
    nq = s // tq
    nkb = s // kblk

    @pl.when(pl.program_id(1) == 0)
    def _():
        r = lax.broadcasted_iota(jnp.int32, tab_s.shape, 1)
        c = lax.broadcasted_iota(jnp.int32, tab_s.shape, 0)
        tab_s[...] = jnp.abs(r - c + (s - tq)).astype(F32) * slope_ref[0, :, 0:1]

    lam_v = lam_ref[...]
    lam = (jnp.exp(jnp.sum(lam_v[0:1] * lam_v[1:2], axis=1, keepdims=True))
           - jnp.exp(jnp.sum(lam_v[2:3] * lam_v[3:4], axis=1, keepdims=True)) + lam_init)

    vt_s[0:A_DV, :] = v_ref[0].astype(F32).T.astype(BF16)
    vt_s[A_DV:, :] = jnp.ones((vt_s.shape[0] - A_DV, s), BF16)
    gain = g_ref[0] * (1.0 - lam_init)
    lane = lax.broadcasted_iota(jnp.int32, (tq, 2 * A_DK), 1)

    def masked_q(u):
        q = q_ref[0, pl.ds(_aligned(u * tq, tq), tq), :]
        zero = jnp.zeros_like(q)
        return jnp.concatenate([jnp.where(lane < A_DK, q, zero), jnp.where(lane >= A_DK, q, zero)], axis=0)

    def score_block(u, qm, kb, mx, dst):
        rows = slice(kb * kblk, (kb + 1) * kblk)
        bias = tab_s[pl.ds(_aligned((nq - 1 - u) * tq + kb * kblk, kblk), kblk), :]
        sc = _dot_nt(k_ref[0, rows, :], qm) - jnp.concatenate([bias, bias], axis=1)
        dst[rows, :] = sc
        return jnp.maximum(mx, jnp.max(sc, axis=0, keepdims=True))

    def pv_block(m, kb, acc, src):
        rows = slice(kb * kblk, (kb + 1) * kblk)
        e = jnp.exp2(src[rows, :] - m)
        return acc + _dot(vt_s[:, rows], e.astype(BF16))

    def finish(u, acc):
        r = 1.0 / acc[A_DV:A_DV + 1, :]
        o = acc[:A_DV, :tq] * r[:, :tq] - lam * (acc[:A_DV, tq:] * r[:, tq:])
        on = o * lax.rsqrt(jnp.mean(o * o, axis=0, keepdims=True) + EPS)
        o_ref[0, pl.ds(_aligned(u * tq, tq), tq), :] = (on.T * gain).astype(o_ref.dtype)

    neg = jnp.full((1, 2 * tq), -jnp.inf, F32)
    acc0 = jnp.zeros((vt_s.shape[0], 2 * tq), F32)

    def stage(u, m_prev, dst, src):
        mx, acc = neg, acc0
        qm = masked_q(u) if dst is not None else None
        for kb in range(nkb):
            if dst is not None:
                mx = score_block(u, qm, kb, mx, dst)
            if src is not None:
                acc = pv_block(m_prev, kb, acc, src)
        if src is not None:
            finish(u - 1, acc)
        return mx

    assert nq % 2 == 0
    m0 = stage(0, None, st0_s, None)

    def body(j, m_even):
        m_odd = stage(2 * j + 1, m_even, st1_s, st0_s)
        return stage(2 * j + 2, m_odd, st0_s, st1_s)

    m_even = lax.fori_loop(0, nq // 2 - 1, body, m0)
    m_odd = stage(nq - 1, m_even, st1_s, st0_s)
    stage(nq, m_odd, None, st1_s)


def _attn_call(za, lam_vec, slopes, g_anorm, lam_init):
    b, s, _ = za.shape
    tq = ATTN_Q_TILE
    blk = 2 * A_DK
    assert blk == A_DV == LANES_V7X
    head_block = lambda off: pl.BlockSpec((1, s, blk), lambda h, i: (i, 0, off + h))
    return pl.pallas_call(
        functools.partial(_attn_kernel, lam_init=lam_init, tq=tq, kblk=ATTN_KEY_BLOCK),
        grid=(A_HEADS, b),
        in_specs=[pl.BlockSpec(lam_vec.shape, lambda h, i: (0, 0)),
                  pl.BlockSpec((1, 1, LANES_V7X), lambda h, i: (h, 0, 0)),
                  pl.BlockSpec((1, 1, A_DV), lambda h, i: (h, 0, 0)),
                  head_block(0), head_block(A_HEADS), head_block(2 * A_HEADS)],
        out_specs=pl.BlockSpec((1, s, A_DV), lambda h, i: (i, 0, h)),
        out_shape=jax.ShapeDtypeStruct((b, s, A_HEADS * A_DV), BF16),
        scratch_shapes=[pltpu.VMEM((2 * s - tq, tq), F32),
                        pltpu.VMEM((A_DV + BF16_ROWS_PER_VREG_V7X, s), BF16),
                        pltpu.VMEM((s, 2 * tq), F32),
                        pltpu.VMEM((s, 2 * tq), F32)],
        compiler_params=pltpu.CompilerParams(
            dimension_semantics=("arbitrary", "arbitrary"), vmem_limit_bytes=VMEM_LIMIT_BYTES_V7X),
        name="diff_attention",
    )(lam_vec, slopes, g_anorm, za, za, za)


def _log_sigmoid(x):
    return jnp.minimum(x, 0.0) - jnp.log(1.0 + jnp.exp(-jnp.abs(x)))


def _chunk_scan(x, op, fill, pos_in_chunk, chunk, reverse):
    n = x.shape[1]
    k = 1
    while k < chunk:
        if reverse:
            shifted = pltpu.roll(x, n - k, 1)
            valid = pos_in_chunk < chunk - k
        else:
            shifted = pltpu.roll(x, k, 1)
            valid = pos_in_chunk >= k
        x = op(x, jnp.where(valid, shifted, fill))
        k *= 2
    return x


def _lane_replicated(row):
    return jnp.broadcast_to(row, (LANES_V7X, row.shape[1])).T


def _mlstm_kernel(zm_ref, zgt_ref, convw_ref, bg_ref, gn_ref, o_ref,
                  qx_s, kt_s, ktb_s, brow_s, cmrow_s, crow_s, c_s, m_s, hf_s, hb_s, *, chunk, rblk):
    s = zm_ref.shape[1]
    nc = s // chunk
    nseq = 2 * M_HEADS
    dqk = M_HEADS * M_DK

    cw = convw_ref[...]
    lane_q = lax.broadcasted_iota(jnp.int32, (rblk, 2 * M_DK), 1)
    for j in range(s // rblk):
        r0 = j * rblk
        halo = SUBLANES_V7X
        parts = []
        if j == 0:
            parts.append(jnp.zeros((halo, 2 * dqk), F32))
        else:
            parts.append(zm_ref[0, r0 - halo:r0, 0:2 * dqk].astype(F32))
        parts.append(zm_ref[0, r0:r0 + rblk, 0:2 * dqk].astype(F32))
        if j == s // rblk - 1:
            parts.append(jnp.zeros((halo, 2 * dqk), F32))
        else:
            parts.append(zm_ref[0, r0 + rblk:r0 + rblk + halo, 0:2 * dqk].astype(F32))
        xe = jnp.concatenate(parts, axis=0)
        ne = rblk + 2 * halo
        y = (pltpu.roll(xe, 1, 0) * cw[0:1] + xe * cw[1:2] + pltpu.roll(xe, ne - 1, 0) * cw[2:3])
        y = y[halo:halo + rblk]
        y = y * jax.nn.sigmoid(y)
        q = y[:, :dqk].astype(BF16)
        kk = y[:, dqk:] * (M_DK ** -0.5)
        kt = kk.T
        kt_s[:, r0:r0 + rblk] = kt
        ktb_s[:, r0:r0 + rblk] = kt.astype(BF16)
        for h in range(M_HEADS):
            pair = q[:, (h // 2) * 2 * M_DK:(h // 2 + 1) * 2 * M_DK]
            keep = (lane_q >= M_DK) if (h % 2) else (lane_q < M_DK)
            qx_s[r0:r0 + rblk, h * 2 * M_DK:(h + 1) * 2 * M_DK] = jnp.where(keep, pair, jnp.zeros_like(pair))

    g = zgt_ref[...] + bg_ref[...]
    li = g[0:nseq] * LOG2E
    lf = _log_sigmoid(g[nseq:2 * nseq]) * LOG2E
    pos = lax.broadcasted_iota(jnp.int32, li.shape, 1) % chunk
    is_fw = lax.broadcasted_iota(jnp.int32, li.shape, 0) < M_HEADS
    b = jnp.where(is_fw, _chunk_scan(lf, jnp.add, 0.0, pos, chunk, False),
                  _chunk_scan(lf, jnp.add, 0.0, pos, chunk, True))
    c = li - b
    cmax = jnp.where(is_fw, _chunk_scan(c, jnp.maximum, -jnp.inf, pos, chunk, False),
                     _chunk_scan(c, jnp.maximum, -jnp.inf, pos, chunk, True))
    for j in range(nc):
        cols = slice(j * chunk, (j + 1) * chunk)
        brow_s[j] = b[:, cols]
        cmrow_s[j] = cmax[:, cols]
        crow_s[j] = c[:, cols]

    c_s[...] = jnp.zeros_like(c_s)
    m_s[...] = jnp.zeros_like(m_s)

    rr = lax.broadcasted_iota(jnp.int32, (chunk, chunk), 0)
    cc = lax.broadcasted_iota(jnp.int32, (chunk, chunk), 1)
    causal = (cc <= rr, cc >= rr)
    ones = jnp.ones((chunk, M_DV), BF16)

    def step(j, carry):
        for d in range(2):
            cj = j if d == 0 else nc - 1 - j
            rows = pl.ds(pl.multiple_of(cj * chunk, chunk), chunk)
            last = chunk - 1 if d == 0 else 0
            h_out = hf_s if d == 0 else hb_s
            for pr in range(M_HEADS // 2):
                state = c_s[d, pr]
                state_bf = state.astype(BF16)
                ktp = ktb_s[pr * 2 * M_DK:(pr + 1) * 2 * M_DK, rows]
                for half in range(2):
                    h = 2 * pr + half
                    i = d * M_HEADS + h
                    blk = slice(h * M_DV, (h + 1) * M_DV)
                    qc = qx_s[rows, blk]
                    v_aug = jnp.concatenate(
                        [zm_ref[0, rows, 2 * dqk + h * M_DV:2 * dqk + (h + 1) * M_DV], ones], axis=1)
                    crow = crow_s[cj, i:i + 1, :]
                    m_old = m_s[i:i + 1, :]
                    b_rep = _lane_replicated(brow_s[cj, i:i + 1, :])
                    m_rep = jnp.maximum(m_old, _lane_replicated(cmrow_s[cj, i:i + 1, :]))
                    dt = jnp.where(causal[d], jnp.exp2(crow - m_rep), 0.0)
                    sm = (_dot(qc, ktp) * dt).astype(BF16)
                    qw = (qc * jnp.exp2(m_old - m_rep)).astype(BF16)
                    tot = _dot(jnp.concatenate([qw, sm], axis=1),
                               jnp.concatenate([state_bf, v_aug], axis=0))
                    floor = jnp.exp2(-(b_rep + m_rep))
                    h_out[rows, blk] = tot[:, :M_DV] / jnp.maximum(jnp.abs(tot[:, M_DV:]), floor)
                    m_last = m_rep[last:last + 1, :]
                    wk = jnp.exp2(crow - m_last[:, 0:1])
                    kth = kt_s[h * M_DK:(h + 1) * M_DK, rows]
                    dc = _dot((kth * wk).astype(BF16), v_aug)
                    sub = slice(half * M_DK, (half + 1) * M_DK)
                    c_s[d, pr, sub, :] = jnp.exp2(m_old - m_last)[:, 0:1] * state[sub] + dc
                    m_s[i:i + 1, :] = b_rep[last:last + 1, :] + m_last
        return carry

    def steps(jj, carry):
        for k in range(MLSTM_CHUNKS_PER_STEP):
            step(jj * MLSTM_CHUNKS_PER_STEP + k, carry)
        return carry

    lax.fori_loop(0, nc // MLSTM_CHUNKS_PER_STEP, steps, 0)

    gn = gn_ref[...]
    dv = M_HEADS * M_DV

    def finish(j, carry):
        rows = pl.ds(pl.multiple_of(j * rblk, rblk), rblk)
        for h in range(M_HEADS):
            blk = slice(h * M_DV, (h + 1) * M_DV)
            hh = hf_s[rows, blk] + hb_s[rows, blk]
            hn = hh * lax.rsqrt(jnp.mean(hh * hh, axis=-1, keepdims=True) + EPS) * gn[:, blk]
            og = zm_ref[0, rows, 2 * dqk + dv + h * M_DV:2 * dqk + dv + (h + 1) * M_DV].astype(F32)
            o_ref[0, rows, blk] = (jax.nn.sigmoid(og) * hn).astype(o_ref.dtype)
        return carry

    lax.fori_loop(0, s // rblk, finish, 0)


def _mlstm_call(zm, zgt, conv_w, bg, g_mnorm):
    b, s, wz = zm.shape
    chunk = MLSTM_CHUNK
    nc = s // chunk
    dv = M_HEADS * M_DV
    dqk = M_HEADS * M_DK
    nseq = 2 * M_HEADS
    full = lambda a: pl.BlockSpec(a.shape, lambda i: (0,) * a.ndim)
    per_b = lambda n: pl.BlockSpec((1, s, n), lambda i: (i, 0, 0))
    gate_rows = lambda: pltpu.VMEM((nc, nseq, chunk), F32)
    return pl.pallas_call(
        functools.partial(_mlstm_kernel, chunk=chunk, rblk=MLSTM_ROW_BLOCK),
        grid=(b,),
        in_specs=[per_b(wz), pl.BlockSpec((zgt.shape[0], s), lambda i: (0, i)), full(conv_w), full(bg),
                  full(g_mnorm)],
        out_specs=per_b(dv),
        out_shape=jax.ShapeDtypeStruct((b, s, dv), BF16),
        scratch_shapes=[
            pltpu.VMEM((s, M_HEADS * 2 * M_DK), BF16),
            pltpu.VMEM((dqk, s), F32),
            pltpu.VMEM((dqk, s), BF16),
            gate_rows(),
            gate_rows(),
            gate_rows(),
            pltpu.VMEM((2, M_HEADS // 2, 2 * M_DK, 2 * M_DV), F32),
            pltpu.VMEM((nseq, LANES_V7X), F32),
            pltpu.VMEM((s, dv), F32),
            pltpu.VMEM((s, dv), F32),
        ],
        compiler_params=pltpu.CompilerParams(
            dimension_semantics=("arbitrary",), vmem_limit_bytes=VMEM_LIMIT_BYTES_V7X),
        name="mlstm_bidir",
    )(zm, zgt, conv_w, bg, g_mnorm)


def _gate_rows(a16):
    hh = M_HEADS
    return jnp.concatenate([a16[..., 0:hh], a16[..., 2 * hh:3 * hh], a16[..., hh:2 * hh], a16[..., 3 * hh:]],
                           axis=-1)


def kernel(x, p, g_ffn1, w_ffn1_in, w_ffn1_out, g_mix, w_in, b_mgate, conv_w, g_mnorm, lam_q1, lam_k1, lam_q2, lam_k2, g_anorm, w_out, g_ffn2, w_ffn2_in, w_ffn2_out, g_ple, w_ple_gate, w_ple_proj, g_final):
    bsz, s, d = x.shape
    depth = w_in.shape[0]
    t = bsz * s
    m_qk, m_v, m_g = M_HEADS * M_DK, M_HEADS * M_DV, 4 * M_HEADS
    a_qk = A_HEADS * 2 * A_DK
    m_end = 2 * m_qk + 2 * m_v
    slopes = np.array([2.0 ** (-8.0 * (h + 1) / A_HEADS) for h in range(A_HEADS)], dtype=np.float32)
    slopes = jnp.broadcast_to(jnp.asarray(slopes * np.float32(LOG2E))[:, None, None],
                              (A_HEADS, 1, LANES_V7X))
    row = lambda v: v.reshape(1, -1).astype(F32)

    h = x.reshape(t, d)
    for i in range(depth):
        lam_init = 0.8 - 0.6 * math.exp(-0.3 * i)
        wi = w_in[i]
        wm = wi[:, :m_end].astype(BF16)
        wa = wi[:, m_end + m_g:]
        wa = jnp.concatenate([wa[:, :a_qk] * (A_DK ** -0.5 * LOG2E), wa[:, a_qk:]], axis=1).astype(BF16)
        wgt = _gate_rows(wi[:, m_end:m_end + m_g]).T.astype(BF16)
        bg = _gate_rows(b_mgate[i].astype(F32)).reshape(m_g, 1)

        h1, zm, za, zgt = _pre_call(h, row(g_ffn1[i]), w_ffn1_in[i].astype(BF16), w_ffn1_out[i].astype(BF16),
                                    row(g_mix[i]), wm, wa, wgt)
        ym = _mlstm_call(zm.reshape(bsz, s, -1), zgt, conv_w[i].astype(F32), bg, row(g_mnorm[i]))
        lam_vec = jnp.stack([lam_q1[i], lam_k1[i], lam_q2[i], lam_k2[i]]).astype(F32)
        ya = _attn_call(za.reshape(bsz, s, -1), lam_vec, slopes,
                        g_anorm[i].reshape(A_HEADS, 1, A_DV).astype(F32), lam_init)
        wo = w_out[i].astype(BF16)
        h = _post_call(h1, ym.reshape(t, -1), ya.reshape(t, -1), p[i].reshape(t, -1),
                       wo[:m_v], wo[m_v:], row(g_ffn2[i]), w_ffn2_in[i].astype(BF16),
                       w_ffn2_out[i].astype(BF16), row(g_ple[i]),
                       w_ple_gate[i].astype(BF16), w_ple_proj[i].astype(BF16), row(g_final),
                       final_norm=(i == depth - 1))
    return h.reshape(bsz, s, d)
```

```python
import functools
import math

import jax
import jax.numpy as jnp
import numpy as np
from jax import lax
from jax.experimental import pallas as pl
from jax.experimental.pallas import tpu as pltpu

F32 = jnp.float32
BF16 = jnp.bfloat16

EPS = 1e-6
LOG2E = 1.4426950408889634

M_HEADS = 4
M_DK = 64
M_DV = 128
A_HEADS = 4
A_DK = 64
A_DV = 128
CONV_K = 3

LANES_V7X = 128
SUBLANES_V7X = 8
BF16_ROWS_PER_VREG_V7X = 16
VMEM_LIMIT_BYTES_V7X = 56 * 1024 * 1024

TOKEN_TILE = 512
FF_CHUNK = 256
MLSTM_CHUNK = 128
MLSTM_CHUNKS_PER_STEP = 4
MLSTM_ROW_BLOCK = 256
ATTN_Q_TILE = 256
ATTN_KEY_BLOCK = 256


def _aligned(x, m):
    return x if isinstance(x, int) else pl.multiple_of(x, m)


def _dot(a, b):
    return jnp.dot(a, b, preferred_element_type=F32)


def _dot_nt(a, b):
    return lax.dot_general(a, b, (((1,), (1,)), ((), ())), preferred_element_type=F32)


def _rmsnorm(x, g):
    ms = jnp.mean(x * x, axis=-1, keepdims=True)
    return x * lax.rsqrt(ms + EPS) * g


def _swiglu(xn, win_ref, wo_ref):
    f = wo_ref.shape[0]
    parts = []
    for c0 in range(0, f, FF_CHUNK):
        g = _dot(xn, win_ref[:, c0:c0 + FF_CHUNK])
        u = _dot(xn, win_ref[:, f + c0:f + c0 + FF_CHUNK])
        parts.append((g * jax.nn.sigmoid(g) * u).astype(BF16))
    return _dot(jnp.concatenate(parts, axis=1), wo_ref[...])


def _resident(shape):
    nd = len(shape)
    return pl.BlockSpec(shape, lambda *_: (0,) * nd, pipeline_mode=pl.Buffered(1))


def _pre_kernel(x_ref, g1_ref, win_ref, wo_ref, gm_ref, wm_ref, wa_ref, wgt_ref,
                h_ref, zm_ref, za_ref, zgt_ref):
    x = x_ref[...]
    xn = _rmsnorm(x, g1_ref[...]).astype(BF16)
    h = x + 0.5 * _swiglu(xn, win_ref, wo_ref)
    h_ref[...] = h
    u = _rmsnorm(h, gm_ref[...]).astype(BF16)
    zm_ref[...] = _dot(u, wm_ref[...]).astype(BF16)
    za_ref[...] = _dot(u, wa_ref[...]).astype(BF16)
    zgt_ref[...] = _dot_nt(wgt_ref[...], u)


def _pre_call(h, g1, win, wo, gm, wm, wa, wgt):
    t, d = h.shape
    tm = TOKEN_TILE
    ng = wgt.shape[0]
    row = lambda n: pl.BlockSpec((tm, n), lambda i: (i, 0))
    weights = (g1, win, wo, gm, wm, wa, wgt)
    return pl.pallas_call(
        _pre_kernel,
        grid=(t // tm,),
        in_specs=[row(d)] + [_resident(w.shape) for w in weights],
        out_specs=[row(d), row(wm.shape[1]), row(wa.shape[1]), pl.BlockSpec((ng, tm), lambda i: (0, i))],
        out_shape=[jax.ShapeDtypeStruct((t, d), F32),
                   jax.ShapeDtypeStruct((t, wm.shape[1]), BF16),
                   jax.ShapeDtypeStruct((t, wa.shape[1]), BF16),
                   jax.ShapeDtypeStruct((ng, t), F32)],
        compiler_params=pltpu.CompilerParams(
            dimension_semantics=("arbitrary",), vmem_limit_bytes=VMEM_LIMIT_BYTES_V7X),
        name="pre_ffn_inproj",
    )(h, *weights)


def _post_kernel(h_ref, ym_ref, ya_ref, p_ref, wom_ref, woa_ref, g2_ref, win_ref, wo_ref,
                 gp_ref, wpg_ref, wpp_ref, gf_ref, o_ref, *, final_norm):
    h = h_ref[...] + _dot(ym_ref[...], wom_ref[...]) + _dot(ya_ref[...], woa_ref[...])
    xn = _rmsnorm(h, g2_ref[...]).astype(BF16)
    h = h + 0.5 * _swiglu(xn, win_ref, wo_ref)
    gate = jax.nn.sigmoid(_dot(_rmsnorm(h, gp_ref[...]).astype(BF16), wpg_ref[...]))
    h = h + gate * _dot(p_ref[...].astype(BF16), wpp_ref[...])
    if final_norm:
        h = _rmsnorm(h, gf_ref[...])
    o_ref[...] = h


def _post_call(h1, ym, ya, p, wom, woa, g2, win, wo, gp, wpg, wpp, gf, final_norm):
    t, d = h1.shape
    tm = TOKEN_TILE
    row = lambda n: pl.BlockSpec((tm, n), lambda i: (i, 0))
    weights = (wom, woa, g2, win, wo, gp, wpg, wpp, gf)
    return pl.pallas_call(
        functools.partial(_post_kernel, final_norm=final_norm),
        grid=(t // tm,),
        in_specs=[row(d), row(ym.shape[1]), row(ya.shape[1]), row(p.shape[1])]
                 + [_resident(w.shape) for w in weights],
        out_specs=row(d),
        out_shape=jax.ShapeDtypeStruct((t, d), F32),
        compiler_params=pltpu.CompilerParams(
            dimension_semantics=("arbitrary",), vmem_limit_bytes=VMEM_LIMIT_BYTES_V7X),
        name="post_outproj_ffn_ple",
    )(h1, ym, ya, p, *weights)


def _attn_kernel(lam_ref, slope_ref, g_ref, q_ref, k_ref, v_ref, qn_ref, kn_ref, o_ref,
                 tab_s, vt_s, qq_s, kk_s, st0_s, st1_s, m0_s, *, lam_init, tq, kblk):
    s = k_ref.shape[1]
    nq = s // tq
    nkb = s // kblk
    first_row = pl.program_id(1) == 0

    @pl.when(first_row)
    def _():
        r = lax.broadcasted_iota(jnp.int32, tab_s.shape, 1)
        c = lax.broadcasted_iota(jnp.int32, tab_s.shape, 0)
        tab_s[...] = jnp.abs(r - c + (s - tq)).astype(F32) * slope_ref[0, :, 0:1]

    lam_v = lam_ref[...]
    lam = (jnp.exp(jnp.sum(lam_v[0:1] * lam_v[1:2], axis=1, keepdims=True))
           - jnp.exp(jnp.sum(lam_v[2:3] * lam_v[3:4], axis=1, keepdims=True)) + lam_init)

    vt_s[0:A_DV, :] = v_ref[0].astype(F32).T.astype(BF16)
    vt_s[A_DV:, :] = jnp.ones((vt_s.shape[0] - A_DV, s), BF16)
    qq_s[0:s, :] = q_ref[0]
    qq_s[s:, :] = qn_ref[0]
    kk_s[0] = k_ref[0]
    kk_s[1] = kn_ref[0]
    gain = g_ref[0] * (1.0 - lam_init)
    lane = lax.broadcasted_iota(jnp.int32, (tq, 2 * A_DK), 1)
    neg = jnp.full((1, 2 * tq), -jnp.inf, F32)
    acc0 = jnp.zeros((vt_s.shape[0], 2 * tq), F32)

    def stage(u, dst, m_prev, src):
        if isinstance(u, int):
            row, tile = u // nq, u % nq
        else:
            row, tile = jnp.where(u == nq, 1, 0), u & (nq - 1)
        q = qq_s[pl.ds(_aligned(u * tq, tq), tq), :]
        zero = jnp.zeros_like(q)
        qm = jnp.concatenate([jnp.where(lane < A_DK, q, zero), jnp.where(lane >= A_DK, q, zero)], axis=0)
        mx, acc = neg, acc0
        for kb in range(nkb):
            rows = slice(kb * kblk, (kb + 1) * kblk)
            bias = tab_s[pl.ds(_aligned((nq - 1 - tile) * tq + kb * kblk, kblk), kblk), :]
            sc = _dot_nt(kk_s[row, rows, :], qm) - jnp.concatenate([bias, bias], axis=1)
            dst[rows, :] = sc
            mx = jnp.maximum(mx, jnp.max(sc, axis=0, keepdims=True))
            if src is not None:
                e = jnp.exp2(src[rows, :] - m_prev)
                acc = acc + _dot(vt_s[:, rows], e.astype(BF16))
        return mx, acc

    def finish(u, acc):
        r = 1.0 / acc[A_DV:A_DV + 1, :]
        o = acc[:A_DV, :tq] * r[:, :tq] - lam * (acc[:A_DV, tq:] * r[:, tq:])
        on = o * lax.rsqrt(jnp.mean(o * o, axis=0, keepdims=True) + EPS)
        o_ref[0, pl.ds(_aligned(u * tq, tq), tq), :] = (on.T * gain).astype(o_ref.dtype)

    assert nq % 2 == 0 and nq & (nq - 1) == 0

    @pl.when(first_row)
    def _():
        m0_s[...] = stage(0, st0_s, None, None)[0]

    m1, acc_0 = stage(1, st1_s, m0_s[...], st0_s)
    finish(0, acc_0)
    carry = stage(2, st0_s, m1, st1_s)

    def body(j, carry):
        m_even, acc_odd = carry
        finish(2 * j - 1, acc_odd)
        m_odd, acc_even = stage(2 * j + 1, st1_s, m_even, st0_s)
        finish(2 * j, acc_even)
        return stage(2 * j + 2, st0_s, m_odd, st1_s)

    m_next, acc_last = lax.fori_loop(1, nq // 2, body, carry)
    finish(nq - 1, acc_last)
    m0_s[...] = m_next


def _attn_call(za, lam_vec, slopes, g_anorm, lam_init):
    b, s, _ = za.shape
    tq = ATTN_Q_TILE
    blk = 2 * A_DK
    assert blk == A_DV == LANES_V7X
    head_block = lambda off: pl.BlockSpec((1, s, blk), lambda h, i: (i, 0, off + h))
    nxt = lambda i: jnp.minimum(i + 1, b - 1)
    return pl.pallas_call(
        functools.partial(_attn_kernel, lam_init=lam_init, tq=tq, kblk=ATTN_KEY_BLOCK),
        grid=(A_HEADS, b),
        in_specs=[pl.BlockSpec(lam_vec.shape, lambda h, i: (0, 0)),
                  pl.BlockSpec((1, 1, LANES_V7X), lambda h, i: (h, 0, 0)),
                  pl.BlockSpec((1, 1, A_DV), lambda h, i: (h, 0, 0)),
                  head_block(0), head_block(A_HEADS), head_block(2 * A_HEADS),
                  pl.BlockSpec((1, tq, blk), lambda h, i: (nxt(i), 0, h)),
                  pl.BlockSpec((1, s, blk), lambda h, i: (nxt(i), 0, A_HEADS + h))],
        out_specs=pl.BlockSpec((1, s, A_DV), lambda h, i: (i, 0, h)),
        out_shape=jax.ShapeDtypeStruct((b, s, A_HEADS * A_DV), BF16),
        scratch_shapes=[pltpu.VMEM((2 * s - tq, tq), F32),
                        pltpu.VMEM((A_DV + BF16_ROWS_PER_VREG_V7X, s), BF16),
                        pltpu.VMEM((s + tq, blk), BF16),
                        pltpu.VMEM((2, s, blk), BF16),
                        pltpu.VMEM((s, 2 * tq), F32),
                        pltpu.VMEM((s, 2 * tq), F32),
                        pltpu.VMEM((1, 2 * tq), F32)],
        compiler_params=pltpu.CompilerParams(
            dimension_semantics=("arbitrary", "arbitrary"), vmem_limit_bytes=VMEM_LIMIT_BYTES_V7X),
        name="diff_attention",
    )(lam_vec, slopes, g_anorm, za, za, za, za, za)


def _log_sigmoid(x):
    return jnp.minimum(x, 0.0) - jnp.log(1.0 + jnp.exp(-jnp.abs(x)))


def _chunk_scan(x, op, fill, pos_in_chunk, chunk, reverse):
    n = x.shape[1]
    k = 1
    while k < chunk:
        if reverse:
            shifted = pltpu.roll(x, n - k, 1)
            valid = pos_in_chunk < chunk - k
        else:
            shifted = pltpu.roll(x, k, 1)
            valid = pos_in_chunk >= k
        x = op(x, jnp.where(valid, shifted, fill))
        k *= 2
    return x


def _lane_replicated(row):
    return jnp.broadcast_to(row, (LANES_V7X, row.shape[1])).T


def _mlstm_kernel(zm_ref, zgt_ref, convw_ref, bg_ref, gn_ref, o_ref,
                  qx_s, kt_s, ktb_s, brow_s, cmrow_s, crow_s, c_s, m_s, hf_s, hb_s, *, chunk, rblk):
    s = zm_ref.shape[1]
    nc = s // chunk
    nseq = 2 * M_HEADS
    dqk = M_HEADS * M_DK

    cw = convw_ref[...]
    lane_q = lax.broadcasted_iota(jnp.int32, (rblk, 2 * M_DK), 1)
    for j in range(s // rblk):
        r0 = j * rblk
        halo = SUBLANES_V7X
        parts = []
        if j == 0:
            parts.append(jnp.zeros((halo, 2 * dqk), F32))
        else:
            parts.append(zm_ref[0, r0 - halo:r0, 0:2 * dqk].astype(F32))
        parts.append(zm_ref[0, r0:r0 + rblk, 0:2 * dqk].astype(F32))
        if j == s // rblk - 1:
            parts.append(jnp.zeros((halo, 2 * dqk), F32))
        else:
            parts.append(zm_ref[0, r0 + rblk:r0 + rblk + halo, 0:2 * dqk].astype(F32))
        xe = jnp.concatenate(parts, axis=0)
        ne = rblk + 2 * halo
        y = (pltpu.roll(xe, 1, 0) * cw[0:1] + xe * cw[1:2] + pltpu.roll(xe, ne - 1, 0) * cw[2:3])
        y = y[halo:halo + rblk]
        y = y * jax.nn.sigmoid(y)
        q = y[:, :dqk].astype(BF16)
        kk = y[:, dqk:] * (M_DK ** -0.5)
        kt = kk.T
        kt_s[:, r0:r0 + rblk] = kt
        ktb_s[:, r0:r0 + rblk] = kt.astype(BF16)
        for h in range(M_HEADS):
            pair = q[:, (h // 2) * 2 * M_DK:(h // 2 + 1) * 2 * M_DK]
            keep = (lane_q >= M_DK) if (h % 2) else (lane_q < M_DK)
            qx_s[r0:r0 + rblk, h * 2 * M_DK:(h + 1) * 2 * M_DK] = jnp.where(keep, pair, jnp.zeros_like(pair))

    g = zgt_ref[...] + bg_ref[...]
    li = g[0:nseq] * LOG2E
    lf = _log_sigmoid(g[nseq:2 * nseq]) * LOG2E
    pos = lax.broadcasted_iota(jnp.int32, li.shape, 1) % chunk
    is_fw = lax.broadcasted_iota(jnp.int32, li.shape, 0) < M_HEADS
    b = jnp.where(is_fw, _chunk_scan(lf, jnp.add, 0.0, pos, chunk, False),
                  _chunk_scan(lf, jnp.add, 0.0, pos, chunk, True))
    c = li - b
    cmax = jnp.where(is_fw, _chunk_scan(c, jnp.maximum, -jnp.inf, pos, chunk, False),
                     _chunk_scan(c, jnp.maximum, -jnp.inf, pos, chunk, True))
    for j in range(nc):
        cols = slice(j * chunk, (j + 1) * chunk)
        brow_s[j] = b[:, cols]
        cmrow_s[j] = cmax[:, cols]
        crow_s[j] = c[:, cols]

    c_s[...] = jnp.zeros_like(c_s)
    m_s[...] = jnp.zeros_like(m_s)

    rr = lax.broadcasted_iota(jnp.int32, (chunk, chunk), 0)
    cc = lax.broadcasted_iota(jnp.int32, (chunk, chunk), 1)
    causal = (cc <= rr, cc >= rr)
    ones = jnp.ones((chunk, M_DV), BF16)

    def step(j, carry):
        for d in range(2):
            cj = j if d == 0 else nc - 1 - j
            rows = pl.ds(pl.multiple_of(cj * chunk, chunk), chunk)
            last = chunk - 1 if d == 0 else 0
            h_out = hf_s if d == 0 else hb_s
            for pr in range(M_HEADS // 2):
                state = c_s[d, pr]
                state_bf = state.astype(BF16)
                ktp = ktb_s[pr * 2 * M_DK:(pr + 1) * 2 * M_DK, rows]
                for half in range(2):
                    h = 2 * pr + half
                    i = d * M_HEADS + h
                    blk = slice(h * M_DV, (h + 1) * M_DV)
                    qc = qx_s[rows, blk]
                    v_aug = jnp.concatenate(
                        [zm_ref[0, rows, 2 * dqk + h * M_DV:2 * dqk + (h + 1) * M_DV], ones], axis=1)
                    crow = crow_s[cj, i:i + 1, :]
                    m_old = m_s[i:i + 1, :]
                    b_rep = _lane_replicated(brow_s[cj, i:i + 1, :])
                    m_rep = jnp.maximum(m_old, _lane_replicated(cmrow_s[cj, i:i + 1, :]))
                    dt = jnp.where(causal[d], jnp.exp2(crow - m_rep), 0.0)
                    sm = (_dot(qc, ktp) * dt).astype(BF16)
                    qw = (qc * jnp.exp2(m_old - m_rep)).astype(BF16)
                    tot = _dot(jnp.concatenate([qw, sm], axis=1),
                               jnp.concatenate([state_bf, v_aug], axis=0))
                    floor = jnp.exp2(-(b_rep + m_rep))
                    h_out[rows, blk] = tot[:, :M_DV] / jnp.maximum(jnp.abs(tot[:, M_DV:]), floor)
                    m_last = m_rep[last:last + 1, :]
                    wk = jnp.exp2(crow - m_last[:, 0:1])
                    kth = kt_s[h * M_DK:(h + 1) * M_DK, rows]
                    dc = _dot((kth * wk).astype(BF16), v_aug)
                    sub = slice(half * M_DK, (half + 1) * M_DK)
                    c_s[d, pr, sub, :] = jnp.exp2(m_old - m_last)[:, 0:1] * state[sub] + dc
                    m_s[i:i + 1, :] = b_rep[last:last + 1, :] + m_last
        return carry

    def steps(jj, carry):
        for k in range(MLSTM_CHUNKS_PER_STEP):
            step(jj * MLSTM_CHUNKS_PER_STEP + k, carry)
        return carry

    lax.fori_loop(0, nc // MLSTM_CHUNKS_PER_STEP, steps, 0)

    gn = gn_ref[...]
    dv = M_HEADS * M_DV

    def finish(j, carry):
        rows = pl.ds(pl.multiple_of(j * rblk, rblk), rblk)
        for h in range(M_HEADS):
            blk = slice(h * M_DV, (h + 1) * M_DV)
            hh = hf_s[rows, blk] + hb_s[rows, blk]
            hn = hh * lax.rsqrt(jnp.mean(hh * hh, axis=-1, keepdims=True) + EPS) * gn[:, blk]
            og = zm_ref[0, rows, 2 * dqk + dv + h * M_DV:2 * dqk + dv + (h + 1) * M_DV].astype(F32)
            o_ref[0, rows, blk] = (jax.nn.sigmoid(og) * hn).astype(o_ref.dtype)
        return carry

    lax.fori_loop(0, s // rblk, finish, 0)


def _mlstm_call(zm, zgt, conv_w, bg, g_mnorm):
    b, s, wz = zm.shape
    chunk = MLSTM_CHUNK
    nc = s // chunk
    dv = M_HEADS * M_DV
    dqk = M_HEADS * M_DK
    nseq = 2 * M_HEADS
    full = lambda a: pl.BlockSpec(a.shape, lambda i: (0,) * a.ndim)
    per_b = lambda n: pl.BlockSpec((1, s, n), lambda i: (i, 0, 0))
    gate_rows = lambda: pltpu.VMEM((nc, nseq, chunk), F32)
    return pl.pallas_call(
        functools.partial(_mlstm_kernel, chunk=chunk, rblk=MLSTM_ROW_BLOCK),
        grid=(b,),
        in_specs=[per_b(wz), pl.BlockSpec((zgt.shape[0], s), lambda i: (0, i)), full(conv_w), full(bg),
                  full(g_mnorm)],
        out_specs=per_b(dv),
        out_shape=jax.ShapeDtypeStruct((b, s, dv), BF16),
        scratch_shapes=[
            pltpu.VMEM((s, M_HEADS * 2 * M_DK), BF16),
            pltpu.VMEM((dqk, s), F32),
            pltpu.VMEM((dqk, s), BF16),
            gate_rows(),
            gate_rows(),
            gate_rows(),
            pltpu.VMEM((2, M_HEADS // 2, 2 * M_DK, 2 * M_DV), F32),
            pltpu.VMEM((nseq, LANES_V7X), F32),
            pltpu.VMEM((s, dv), F32),
            pltpu.VMEM((s, dv), F32),
        ],
        compiler_params=pltpu.CompilerParams(
            dimension_semantics=("arbitrary",), vmem_limit_bytes=VMEM_LIMIT_BYTES_V7X),
        name="mlstm_bidir",
    )(zm, zgt, conv_w, bg, g_mnorm)


def _gate_rows(a16):
    hh = M_HEADS
    return jnp.concatenate([a16[..., 0:hh], a16[..., 2 * hh:3 * hh], a16[..., hh:2 * hh], a16[..., 3 * hh:]],
                           axis=-1)


def kernel(x, p, g_ffn1, w_ffn1_in, w_ffn1_out, g_mix, w_in, b_mgate, conv_w, g_mnorm, lam_q1, lam_k1, lam_q2, lam_k2, g_anorm, w_out, g_ffn2, w_ffn2_in, w_ffn2_out, g_ple, w_ple_gate, w_ple_proj, g_final):
    bsz, s, d = x.shape
    depth = w_in.shape[0]
    t = bsz * s
    m_qk, m_v, m_g = M_HEADS * M_DK, M_HEADS * M_DV, 4 * M_HEADS
    a_qk = A_HEADS * 2 * A_DK
    m_end = 2 * m_qk + 2 * m_v
    slopes = np.array([2.0 ** (-8.0 * (h + 1) / A_HEADS) for h in range(A_HEADS)], dtype=np.float32)
    slopes = jnp.broadcast_to(jnp.asarray(slopes * np.float32(LOG2E))[:, None, None],
                              (A_HEADS, 1, LANES_V7X))
    row = lambda v: v.reshape(1, -1).astype(F32)

    h = x.reshape(t, d)
    for i in range(depth):
        lam_init = 0.8 - 0.6 * math.exp(-0.3 * i)
        wi = w_in[i]
        wm = wi[:, :m_end].astype(BF16)
        wa = wi[:, m_end + m_g:]
        wa = jnp.concatenate([wa[:, :a_qk] * (A_DK ** -0.5 * LOG2E), wa[:, a_qk:]], axis=1).astype(BF16)
        wgt = _gate_rows(wi[:, m_end:m_end + m_g]).T.astype(BF16)
        bg = _gate_rows(b_mgate[i].astype(F32)).reshape(m_g, 1)

        h1, zm, za, zgt = _pre_call(h, row(g_ffn1[i]), w_ffn1_in[i].astype(BF16), w_ffn1_out[i].astype(BF16),
                                    row(g_mix[i]), wm, wa, wgt)
        ym = _mlstm_call(zm.reshape(bsz, s, -1), zgt, conv_w[i].astype(F32), bg, row(g_mnorm[i]))
        lam_vec = jnp.stack([lam_q1[i], lam_k1[i], lam_q2[i], lam_k2[i]]).astype(F32)
        ya = _attn_call(za.reshape(bsz, s, -1), lam_vec, slopes,
                        g_anorm[i].reshape(A_HEADS, 1, A_DV).astype(F32), lam_init)
        wo = w_out[i].astype(BF16)
        h = _post_call(h1, ym.reshape(t, -1), ya.reshape(t, -1), p[i].reshape(t, -1),
                       wo[:m_v], wo[m_v:], row(g_ffn2[i]), w_ffn2_in[i].astype(BF16),
                       w_ffn2_out[i].astype(BF16), row(g_ple[i]),
                       w_ple_gate[i].astype(BF16), w_ple_proj[i].astype(BF16), row(g_final),
                       final_norm=(i == depth - 1))
    return h.reshape(bsz, s, d)
```

```python
import functools
import math

import jax
import jax.numpy as jnp
import numpy as np
from jax import lax
from jax.experimental import pallas as pl
from jax.experimental.pallas import tpu as pltpu

F32 = jnp.float32
BF16 = jnp.bfloat16

EPS = 1e-6
LOG2E = 1.4426950408889634

M_HEADS = 4
M_DK = 64
M_DV = 128
A_HEADS = 4
A_DK = 64
A_DV = 128
CONV_K = 3

LANES_V7X = 128
SUBLANES_V7X = 8
BF16_ROWS_PER_VREG_V7X = 16
VMEM_LIMIT_BYTES_V7X = 56 * 1024 * 1024

TOKEN_TILE = 512
FF_CHUNK = 256
MLSTM_CHUNK = 128
MLSTM_CHUNKS_PER_STEP = 4
MLSTM_ROW_BLOCK = 256
ATTN_Q_TILE = 256
ATTN_KEY_BLOCK = 256


def _aligned(x, m):
    return x if isinstance(x, int) else pl.multiple_of(x, m)


def _dot(a, b):
    return jnp.dot(a, b, preferred_element_type=F32)


def _dot_nt(a, b):
    return lax.dot_general(a, b, (((1,), (1,)), ((), ())), preferred_element_type=F32)


def _rmsnorm(x, g):
    ms = jnp.mean(x * x, axis=-1, keepdims=True)
    return x * lax.rsqrt(ms + EPS) * g


def _swiglu(xn, win_ref, wo_ref):
    f = wo_ref.shape[0]
    parts = []
    for c0 in range(0, f, FF_CHUNK):
        g = _dot(xn, win_ref[:, c0:c0 + FF_CHUNK])
        u = _dot(xn, win_ref[:, f + c0:f + c0 + FF_CHUNK])
        parts.append((g * jax.nn.sigmoid(g) * u).astype(BF16))
    return _dot(jnp.concatenate(parts, axis=1), wo_ref[...])


def _resident(shape):
    nd = len(shape)
    return pl.BlockSpec(shape, lambda *_: (0,) * nd, pipeline_mode=pl.Buffered(1))


def _pre_kernel(x_ref, g1_ref, win_ref, wo_ref, gm_ref, wm_ref, wa_ref, wgt_ref, *rest):
    n_cast = (len(rest) - 4) // 2
    cast_in, (h_ref, zm_ref, za_ref, zgt_ref), cast_out = rest[:n_cast], rest[n_cast:n_cast + 4], rest[n_cast + 4:]
    x = x_ref[...]
    xn = _rmsnorm(x, g1_ref[...]).astype(BF16)
    h = x + 0.5 * _swiglu(xn, win_ref, wo_ref)
    h_ref[...] = h
    u = _rmsnorm(h, gm_ref[...]).astype(BF16)
    zm_ref[...] = _dot(u, wm_ref[...]).astype(BF16)
    za_ref[...] = _dot(u, wa_ref[...]).astype(BF16)
    zgt_ref[...] = _dot_nt(wgt_ref[...], u)
    for src, dst in zip(cast_in, cast_out):
        dst[...] = src[...].astype(BF16)


def _pre_call(h, g1, win, wo, gm, wm, wa, wgt, later_weights):
    t, d = h.shape
    tm = TOKEN_TILE
    ng = wgt.shape[0]
    steps = t // tm
    row = lambda n: pl.BlockSpec((tm, n), lambda i: (i, 0))
    weights = (g1, win, wo, gm, wm, wa, wgt)
    slabs = []
    for w in later_weights:
        assert w.shape[0] % (steps * BF16_ROWS_PER_VREG_V7X) == 0, w.shape
        slabs.append(pl.BlockSpec((w.shape[0] // steps, w.shape[1]), lambda i: (i, 0)))
    outs = pl.pallas_call(
        _pre_kernel,
        grid=(steps,),
        in_specs=[row(d)] + [_resident(w.shape) for w in weights] + slabs,
        out_specs=[row(d), row(wm.shape[1]), row(wa.shape[1]), pl.BlockSpec((ng, tm), lambda i: (0, i))] + slabs,
        out_shape=[jax.ShapeDtypeStruct((t, d), F32),
                   jax.ShapeDtypeStruct((t, wm.shape[1]), BF16),
                   jax.ShapeDtypeStruct((t, wa.shape[1]), BF16),
                   jax.ShapeDtypeStruct((ng, t), F32)]
                  + [jax.ShapeDtypeStruct(w.shape, BF16) for w in later_weights],
        compiler_params=pltpu.CompilerParams(
            dimension_semantics=("arbitrary",), vmem_limit_bytes=VMEM_LIMIT_BYTES_V7X),
        name="pre_ffn_inproj",
    )(h, *weights, *later_weights)
    return outs[:4], outs[4:]


def _post_kernel(h_ref, ym_ref, ya_ref, p_ref, wout_ref, g2_ref, win_ref, wo_ref,
                 gp_ref, wpg_ref, wpp_ref, gf_ref, o_ref, *, final_norm):
    dm = ym_ref.shape[1]
    h = h_ref[...] + _dot(ym_ref[...], wout_ref[0:dm, :]) + _dot(ya_ref[...], wout_ref[dm:, :])
    xn = _rmsnorm(h, g2_ref[...]).astype(BF16)
    h = h + 0.5 * _swiglu(xn, win_ref, wo_ref)
    gate = jax.nn.sigmoid(_dot(_rmsnorm(h, gp_ref[...]).astype(BF16), wpg_ref[...]))
    h = h + gate * _dot(p_ref[...].astype(BF16), wpp_ref[...])
    if final_norm:
        h = _rmsnorm(h, gf_ref[...])
    o_ref[...] = h


def _post_call(h1, ym, ya, p, wout, g2, win, wo, gp, wpg, wpp, gf, final_norm):
    t, d = h1.shape
    tm = TOKEN_TILE
    row = lambda n: pl.BlockSpec((tm, n), lambda i: (i, 0))
    weights = (wout, g2, win, wo, gp, wpg, wpp, gf)
    return pl.pallas_call(
        functools.partial(_post_kernel, final_norm=final_norm),
        grid=(t // tm,),
        in_specs=[row(d), row(ym.shape[1]), row(ya.shape[1]), row(p.shape[1])]
                 + [_resident(w.shape) for w in weights],
        out_specs=row(d),
        out_shape=jax.ShapeDtypeStruct((t, d), F32),
        compiler_params=pltpu.CompilerParams(
            dimension_semantics=("arbitrary",), vmem_limit_bytes=VMEM_LIMIT_BYTES_V7X),
        name="post_outproj_ffn_ple",
    )(h1, ym, ya, p, *weights)


def _attn_kernel(lam_ref, slope_ref, g_ref, q_ref, k_ref, v_ref, qn_ref, kn_ref, o_ref,
                 tab_s, vt_s, qq_s, kk_s, st0_s, st1_s, m0_s, *, lam_init, tq, kblk):
    s = k_ref.shape[1]
    nq = s // tq
    nkb = s // kblk
    first_row = pl.program_id(1) == 0

    @pl.when(first_row)
    def _():
        r = lax.broadcasted_iota(jnp.int32, tab_s.shape, 1)
        c = lax.broadcasted_iota(jnp.int32, tab_s.shape, 0)
        tab_s[...] = jnp.abs(r - c + (s - tq)).astype(F32) * slope_ref[0, :, 0:1]

    lam_v = lam_ref[...]
    lam = (jnp.exp(jnp.sum(lam_v[0:1] * lam_v[1:2], axis=1, keepdims=True))
           - jnp.exp(jnp.sum(lam_v[2:3] * lam_v[3:4], axis=1, keepdims=True)) + lam_init)

    vt_s[0:A_DV, :] = v_ref[0].astype(F32).T.astype(BF16)
    vt_s[A_DV:, :] = jnp.ones((vt_s.shape[0] - A_DV, s), BF16)
    qq_s[0:s, :] = q_ref[0]
    qq_s[s:, :] = qn_ref[0]
    kk_s[0] = k_ref[0]
    kk_s[1] = kn_ref[0]
    gain = g_ref[0] * (1.0 - lam_init)
    lane = lax.broadcasted_iota(jnp.int32, (tq, 2 * A_DK), 1)
    neg = jnp.full((1, 2 * tq), -jnp.inf, F32)
    acc0 = jnp.zeros((vt_s.shape[0], 2 * tq), F32)

    def stage(u, dst, m_prev, src):
        if isinstance(u, int):
            row, tile = u // nq, u % nq
        else:
            row, tile = jnp.where(u == nq, 1, 0), u & (nq - 1)
        q = qq_s[pl.ds(_aligned(u * tq, tq), tq), :]
        zero = jnp.zeros_like(q)
        qm = jnp.concatenate([jnp.where(lane < A_DK, q, zero), jnp.where(lane >= A_DK, q, zero)], axis=0)
        mx, acc = neg, acc0
        for kb in range(nkb):
            rows = slice(kb * kblk, (kb + 1) * kblk)
            bias = tab_s[pl.ds(_aligned((nq - 1 - tile) * tq + kb * kblk, kblk), kblk), :]
            sc = _dot_nt(kk_s[row, rows, :], qm) - jnp.concatenate([bias, bias], axis=1)
            dst[rows, :] = sc
            mx = jnp.maximum(mx, jnp.max(sc, axis=0, keepdims=True))
            if src is not None:
                e = jnp.exp2(src[rows, :] - m_prev)
                acc = acc + _dot(vt_s[:, rows], e.astype(BF16))
        return mx, acc

    def finish(u, acc):
        r = 1.0 / acc[A_DV:A_DV + 1, :]
        o = acc[:A_DV, :tq] * r[:, :tq] - lam * (acc[:A_DV, tq:] * r[:, tq:])
        on = o * lax.rsqrt(jnp.mean(o * o, axis=0, keepdims=True) + EPS)
        o_ref[0, pl.ds(_aligned(u * tq, tq), tq), :] = (on.T * gain).astype(o_ref.dtype)

    assert nq % 2 == 0 and nq & (nq - 1) == 0

    @pl.when(first_row)
    def _():
        m0_s[...] = stage(0, st0_s, None, None)[0]

    m1, acc_0 = stage(1, st1_s, m0_s[...], st0_s)
    finish(0, acc_0)
    carry = stage(2, st0_s, m1, st1_s)

    def body(j, carry):
        m_even, acc_odd = carry
        finish(2 * j - 1, acc_odd)
        m_odd, acc_even = stage(2 * j + 1, st1_s, m_even, st0_s)
        finish(2 * j, acc_even)
        return stage(2 * j + 2, st0_s, m_odd, st1_s)

    m_next, acc_last = lax.fori_loop(1, nq // 2, body, carry)
    finish(nq - 1, acc_last)
    m0_s[...] = m_next


def _attn_call(za, lam_vec, slopes, g_anorm, lam_init):
    b, s, _ = za.shape
    tq = ATTN_Q_TILE
    blk = 2 * A_DK
    assert blk == A_DV == LANES_V7X
    head_block = lambda off: pl.BlockSpec((1, s, blk), lambda h, i: (i, 0, off + h))
    nxt = lambda i: jnp.minimum(i + 1, b - 1)
    return pl.pallas_call(
        functools.partial(_attn_kernel, lam_init=lam_init, tq=tq, kblk=ATTN_KEY_BLOCK),
        grid=(A_HEADS, b),
        in_specs=[pl.BlockSpec(lam_vec.shape, lambda h, i: (0, 0)),
                  pl.BlockSpec((1, 1, LANES_V7X), lambda h, i: (h, 0, 0)),
                  pl.BlockSpec((1, 1, A_DV), lambda h, i: (h, 0, 0)),
                  head_block(0), head_block(A_HEADS), head_block(2 * A_HEADS),
                  pl.BlockSpec((1, tq, blk), lambda h, i: (nxt(i), 0, h)),
                  pl.BlockSpec((1, s, blk), lambda h, i: (nxt(i), 0, A_HEADS + h))],
        out_specs=pl.BlockSpec((1, s, A_DV), lambda h, i: (i, 0, h)),
        out_shape=jax.ShapeDtypeStruct((b, s, A_HEADS * A_DV), BF16),
        scratch_shapes=[pltpu.VMEM((2 * s - tq, tq), F32),
                        pltpu.VMEM((A_DV + BF16_ROWS_PER_VREG_V7X, s), BF16),
                        pltpu.VMEM((s + tq, blk), BF16),
                        pltpu.VMEM((2, s, blk), BF16),
                        pltpu.VMEM((s, 2 * tq), F32),
                        pltpu.VMEM((s, 2 * tq), F32),
                        pltpu.VMEM((1, 2 * tq), F32)],
        compiler_params=pltpu.CompilerParams(
            dimension_semantics=("arbitrary", "arbitrary"), vmem_limit_bytes=VMEM_LIMIT_BYTES_V7X),
        name="diff_attention",
    )(lam_vec, slopes, g_anorm, za, za, za, za, za)


def _log_sigmoid(x):
    return jnp.minimum(x, 0.0) - jnp.log(1.0 + jnp.exp(-jnp.abs(x)))


def _chunk_scan(x, op, fill, pos_in_chunk, chunk, reverse):
    n = x.shape[1]
    k = 1
    while k < chunk:
        if reverse:
            shifted = pltpu.roll(x, n - k, 1)
            valid = pos_in_chunk < chunk - k
        else:
            shifted = pltpu.roll(x, k, 1)
            valid = pos_in_chunk >= k
        x = op(x, jnp.where(valid, shifted, fill))
        k *= 2
    return x


def _lane_replicated(row):
    return jnp.broadcast_to(row, (LANES_V7X, row.shape[1])).T


def _mlstm_kernel(zm_ref, zgt_ref, convw_ref, bg_ref, gn_ref, o_ref,
                  qx_s, kt_s, ktb_s, brow_s, cmrow_s, crow_s, c_s, m_s, hf_s, hb_s, *, chunk, rblk):
    s = zm_ref.shape[1]
    nc = s // chunk
    nseq = 2 * M_HEADS
    dqk = M_HEADS * M_DK

    cw = convw_ref[...]
    lane_q = lax.broadcasted_iota(jnp.int32, (rblk, 2 * M_DK), 1)
    for j in range(s // rblk):
        r0 = j * rblk
        halo = SUBLANES_V7X
        parts = []
        if j == 0:
            parts.append(jnp.zeros((halo, 2 * dqk), F32))
        else:
            parts.append(zm_ref[0, r0 - halo:r0, 0:2 * dqk].astype(F32))
        parts.append(zm_ref[0, r0:r0 + rblk, 0:2 * dqk].astype(F32))
        if j == s // rblk - 1:
            parts.append(jnp.zeros((halo, 2 * dqk), F32))
        else:
            parts.append(zm_ref[0, r0 + rblk:r0 + rblk + halo, 0:2 * dqk].astype(F32))
        xe = jnp.concatenate(parts, axis=0)
        ne = rblk + 2 * halo
        y = (pltpu.roll(xe, 1, 0) * cw[0:1] + xe * cw[1:2] + pltpu.roll(xe, ne - 1, 0) * cw[2:3])
        y = y[halo:halo + rblk]
        y = y * jax.nn.sigmoid(y)
        q = y[:, :dqk].astype(BF16)
        kk = y[:, dqk:] * (M_DK ** -0.5)
        kt = kk.T
        kt_s[:, r0:r0 + rblk] = kt
        ktb_s[:, r0:r0 + rblk] = kt.astype(BF16)
        for h in range(M_HEADS):
            pair = q[:, (h // 2) * 2 * M_DK:(h // 2 + 1) * 2 * M_DK]
            keep = (lane_q >= M_DK) if (h % 2) else (lane_q < M_DK)
            qx_s[r0:r0 + rblk, h * 2 * M_DK:(h + 1) * 2 * M_DK] = jnp.where(keep, pair, jnp.zeros_like(pair))

    g = zgt_ref[...] + bg_ref[...]
    li = g[0:nseq] * LOG2E
    lf = _log_sigmoid(g[nseq:2 * nseq]) * LOG2E
    pos = lax.broadcasted_iota(jnp.int32, li.shape, 1) % chunk
    is_fw = lax.broadcasted_iota(jnp.int32, li.shape, 0) < M_HEADS
    b = jnp.where(is_fw, _chunk_scan(lf, jnp.add, 0.0, pos, chunk, False),
                  _chunk_scan(lf, jnp.add, 0.0, pos, chunk, True))
    c = li - b
    cmax = jnp.where(is_fw, _chunk_scan(c, jnp.maximum, -jnp.inf, pos, chunk, False),
                     _chunk_scan(c, jnp.maximum, -jnp.inf, pos, chunk, True))
    for j in range(nc):
        cols = slice(j * chunk, (j + 1) * chunk)
        brow_s[j] = b[:, cols]
        cmrow_s[j] = cmax[:, cols]
        crow_s[j] = c[:, cols]

    c_s[...] = jnp.zeros_like(c_s)
    m_s[...] = jnp.zeros_like(m_s)

    rr = lax.broadcasted_iota(jnp.int32, (chunk, chunk), 0)
    cc = lax.broadcasted_iota(jnp.int32, (chunk, chunk), 1)
    causal = (cc <= rr, cc >= rr)
    ones = jnp.ones((chunk, M_DV), BF16)

    def step(j, carry):
        for d in range(2):
            cj = j if d == 0 else nc - 1 - j
            rows = pl.ds(pl.multiple_of(cj * chunk, chunk), chunk)
            last = chunk - 1 if d == 0 else 0
            h_out = hf_s if d == 0 else hb_s
            for pr in range(M_HEADS // 2):
                state = c_s[d, pr]
                state_bf = state.astype(BF16)
                ktp = ktb_s[pr * 2 * M_DK:(pr + 1) * 2 * M_DK, rows]
                for half in range(2):
                    h = 2 * pr + half
                    i = d * M_HEADS + h
                    blk = slice(h * M_DV, (h + 1) * M_DV)
                    qc = qx_s[rows, blk]
                    v_aug = jnp.concatenate(
                        [zm_ref[0, rows, 2 * dqk + h * M_DV:2 * dqk + (h + 1) * M_DV], ones], axis=1)
                    crow = crow_s[cj, i:i + 1, :]
                    m_old = m_s[i:i + 1, :]
                    b_rep = _lane_replicated(brow_s[cj, i:i + 1, :])
                    m_rep = jnp.maximum(m_old, _lane_replicated(cmrow_s[cj, i:i + 1, :]))
                    dt = jnp.where(causal[d], jnp.exp2(crow - m_rep), 0.0)
                    sm = (_dot(qc, ktp) * dt).astype(BF16)
                    qw = (qc * jnp.exp2(m_old - m_rep)).astype(BF16)
                    tot = _dot(jnp.concatenate([qw, sm], axis=1),
                               jnp.concatenate([state_bf, v_aug], axis=0))
                    floor = jnp.exp2(-(b_rep + m_rep))
                    h_out[rows, blk] = tot[:, :M_DV] / jnp.maximum(jnp.abs(tot[:, M_DV:]), floor)
                    m_last = m_rep[last:last + 1, :]
                    wk = jnp.exp2(crow - m_last[:, 0:1])
                    kth = kt_s[h * M_DK:(h + 1) * M_DK, rows]
                    dc = _dot((kth * wk).astype(BF16), v_aug)
                    sub = slice(half * M_DK, (half + 1) * M_DK)
                    c_s[d, pr, sub, :] = jnp.exp2(m_old - m_last)[:, 0:1] * state[sub] + dc
                    m_s[i:i + 1, :] = b_rep[last:last + 1, :] + m_last
        return carry

    def steps(jj, carry):
        for k in range(MLSTM_CHUNKS_PER_STEP):
            step(jj * MLSTM_CHUNKS_PER_STEP + k, carry)
        return carry

    lax.fori_loop(0, nc // MLSTM_CHUNKS_PER_STEP, steps, 0)

    gn = gn_ref[...]
    dv = M_HEADS * M_DV

    def finish(j, carry):
        rows = pl.ds(pl.multiple_of(j * rblk, rblk), rblk)
        for h in range(M_HEADS):
            blk = slice(h * M_DV, (h + 1) * M_DV)
            hh = hf_s[rows, blk] + hb_s[rows, blk]
            hn = hh * lax.rsqrt(jnp.mean(hh * hh, axis=-1, keepdims=True) + EPS) * gn[:, blk]
            og = zm_ref[0, rows, 2 * dqk + dv + h * M_DV:2 * dqk + dv + (h + 1) * M_DV].astype(F32)
            o_ref[0, rows, blk] = (jax.nn.sigmoid(og) * hn).astype(o_ref.dtype)
        return carry

    lax.fori_loop(0, s // rblk, finish, 0)


def _mlstm_call(zm, zgt, conv_w, bg, g_mnorm):
    b, s, wz = zm.shape
    chunk = MLSTM_CHUNK
    nc = s // chunk
    dv = M_HEADS * M_DV
    dqk = M_HEADS * M_DK
    nseq = 2 * M_HEADS
    full = lambda a: pl.BlockSpec(a.shape, lambda i: (0,) * a.ndim)
    per_b = lambda n: pl.BlockSpec((1, s, n), lambda i: (i, 0, 0))
    gate_rows = lambda: pltpu.VMEM((nc, nseq, chunk), F32)
    return pl.pallas_call(
        functools.partial(_mlstm_kernel, chunk=chunk, rblk=MLSTM_ROW_BLOCK),
        grid=(b,),
        in_specs=[per_b(wz), pl.BlockSpec((zgt.shape[0], s), lambda i: (0, i)), full(conv_w), full(bg),
                  full(g_mnorm)],
        out_specs=per_b(dv),
        out_shape=jax.ShapeDtypeStruct((b, s, dv), BF16),
        scratch_shapes=[
            pltpu.VMEM((s, M_HEADS * 2 * M_DK), BF16),
            pltpu.VMEM((dqk, s), F32),
            pltpu.VMEM((dqk, s), BF16),
            gate_rows(),
            gate_rows(),
            gate_rows(),
            pltpu.VMEM((2, M_HEADS // 2, 2 * M_DK, 2 * M_DV), F32),
            pltpu.VMEM((nseq, LANES_V7X), F32),
            pltpu.VMEM((s, dv), F32),
            pltpu.VMEM((s, dv), F32),
        ],
        compiler_params=pltpu.CompilerParams(
            dimension_semantics=("arbitrary",), vmem_limit_bytes=VMEM_LIMIT_BYTES_V7X),
        name="mlstm_bidir",
    )(zm, zgt, conv_w, bg, g_mnorm)


def _gate_rows(a16):
    hh = M_HEADS
    return jnp.concatenate([a16[..., 0:hh], a16[..., 2 * hh:3 * hh], a16[..., hh:2 * hh], a16[..., 3 * hh:]],
                           axis=-1)


def kernel(x, p, g_ffn1, w_ffn1_in, w_ffn1_out, g_mix, w_in, b_mgate, conv_w, g_mnorm, lam_q1, lam_k1, lam_q2, lam_k2, g_anorm, w_out, g_ffn2, w_ffn2_in, w_ffn2_out, g_ple, w_ple_gate, w_ple_proj, g_final):
    bsz, s, d = x.shape
    depth = w_in.shape[0]
    t = bsz * s
    m_qk, m_v, m_g = M_HEADS * M_DK, M_HEADS * M_DV, 4 * M_HEADS
    a_qk = A_HEADS * 2 * A_DK
    m_end = 2 * m_qk + 2 * m_v
    slopes = np.array([2.0 ** (-8.0 * (h + 1) / A_HEADS) for h in range(A_HEADS)], dtype=np.float32)
    slopes = jnp.broadcast_to(jnp.asarray(slopes * np.float32(LOG2E))[:, None, None],
                              (A_HEADS, 1, LANES_V7X))
    row = lambda v: v.reshape(1, -1).astype(F32)

    h = x.reshape(t, d)
    for i in range(depth):
        lam_init = 0.8 - 0.6 * math.exp(-0.3 * i)
        wi = w_in[i]
        wm = wi[:, :m_end].astype(BF16)
        wa = wi[:, m_end + m_g:]
        wa = jnp.concatenate([wa[:, :a_qk] * (A_DK ** -0.5 * LOG2E), wa[:, a_qk:]], axis=1).astype(BF16)
        wgt = _gate_rows(wi[:, m_end:m_end + m_g]).T.astype(BF16)
        bg = _gate_rows(b_mgate[i].astype(F32)).reshape(m_g, 1)

        later = (w_ffn2_in[i], w_ffn2_out[i].reshape(d, -1), w_out[i], w_ple_gate[i], w_ple_proj[i].reshape(d, -1))
        (h1, zm, za, zgt), (win2, wo2, wout, wpg, wpp) = _pre_call(
            h, row(g_ffn1[i]), w_ffn1_in[i].astype(BF16), w_ffn1_out[i].astype(BF16), row(g_mix[i]), wm, wa, wgt,
            later)
        ym = _mlstm_call(zm.reshape(bsz, s, -1), zgt, conv_w[i].astype(F32), bg, row(g_mnorm[i]))
        lam_vec = jnp.stack([lam_q1[i], lam_k1[i], lam_q2[i], lam_k2[i]]).astype(F32)
        ya = _attn_call(za.reshape(bsz, s, -1), lam_vec, slopes,
                        g_anorm[i].reshape(A_HEADS, 1, A_DV).astype(F32), lam_init)
        h = _post_call(h1, ym.reshape(t, -1), ya.reshape(t, -1), p[i].reshape(t, -1),
                       wout, row(g_ffn2[i]), win2, wo2.reshape(w_ffn2_out[i].shape), row(g_ple[i]),
                       wpg, wpp.reshape(w_ple_proj[i].shape), row(g_final),
                       final_norm=(i == depth - 1))
    return h.reshape(bsz, s, d)
```

```python
import functools
import math

import jax
import jax.numpy as jnp
import numpy as np
from jax import lax
from jax.experimental import pallas as pl
from jax.experimental.pallas import tpu as pltpu

F32 = jnp.float32
BF16 = jnp.bfloat16

EPS = 1e-6
LOG2E = 1.4426950408889634

M_HEADS = 4
M_DK = 64
M_DV = 128
A_HEADS = 4
A_DK = 64
A_DV = 128
CONV_K = 3

LANES_V7X = 128
SUBLANES_V7X = 8
BF16_ROWS_PER_VREG_V7X = 16
VMEM_LIMIT_BYTES_V7X = 56 * 1024 * 1024

TOKEN_TILE = 512
FF_CHUNK = 256
MLSTM_CHUNK = 128
MLSTM_CHUNKS_PER_STEP = 4
MLSTM_ROW_BLOCK = 256
ATTN_Q_TILE = 256
ATTN_KEY_BLOCK = 256


def _aligned(x, m):
    return x if isinstance(x, int) else pl.multiple_of(x, m)


def _dot(a, b):
    return jnp.dot(a, b, preferred_element_type=F32)


def _dot_nt(a, b):
    return lax.dot_general(a, b, (((1,), (1,)), ((), ())), preferred_element_type=F32)


def _rmsnorm(x, g):
    ms = jnp.mean(x * x, axis=-1, keepdims=True)
    return x * lax.rsqrt(ms + EPS) * g


def _swiglu(xn, win_ref, wo_ref):
    f = wo_ref.shape[0]
    parts = []
    for c0 in range(0, f, FF_CHUNK):
        g = _dot(xn, win_ref[:, c0:c0 + FF_CHUNK])
        u = _dot(xn, win_ref[:, f + c0:f + c0 + FF_CHUNK])
        parts.append((g * jax.nn.sigmoid(g) * u).astype(BF16))
    return _dot(jnp.concatenate(parts, axis=1), wo_ref[...])


def _resident(shape):
    nd = len(shape)
    return pl.BlockSpec(shape, lambda *_: (0,) * nd, pipeline_mode=pl.Buffered(1))


def _pre_kernel(x_ref, g1_ref, win_ref, wo_ref, gm_ref, wm_ref, wa_ref, wgt_ref, *rest):
    n_cast = (len(rest) - 4) // 2
    cast_in, (h_ref, zm_ref, za_ref, zgt_ref), cast_out = rest[:n_cast], rest[n_cast:n_cast + 4], rest[n_cast + 4:]
    x = x_ref[...]
    xn = _rmsnorm(x, g1_ref[...]).astype(BF16)
    h = x + 0.5 * _swiglu(xn, win_ref, wo_ref)
    h_ref[...] = h
    u = _rmsnorm(h, gm_ref[...]).astype(BF16)
    zm_ref[...] = _dot(u, wm_ref[...]).astype(BF16)
    za_ref[...] = _dot(u, wa_ref[...]).astype(BF16)
    zgt_ref[...] = _dot_nt(wgt_ref[...], u)
    for src, dst in zip(cast_in, cast_out):
        dst[...] = src[0].astype(BF16)


def _slab_rows(rows, steps):
    for r in range(BF16_ROWS_PER_VREG_V7X, rows + 1, BF16_ROWS_PER_VREG_V7X):
        if rows % r == 0 and rows // r <= steps:
            return r
    raise ValueError((rows, steps))


def _pre_call(h, g1, win, wo, gm, wm, wa, wgt, layer, later_weights):
    t, d = h.shape
    tm = TOKEN_TILE
    ng = wgt.shape[0]
    steps = t // tm
    row = lambda n: pl.BlockSpec((tm, n), lambda i: (i, 0))
    weights = (g1, win, wo, gm, wm, wa, wgt)
    slabs_in, slabs_out = [], []
    for w in later_weights:
        r = _slab_rows(w.shape[1], steps)
        last = w.shape[1] // r - 1
        slabs_in.append(pl.BlockSpec((1, r, w.shape[2]), lambda i, last=last: (layer, jnp.minimum(i, last), 0)))
        slabs_out.append(pl.BlockSpec((r, w.shape[2]), lambda i, last=last: (jnp.minimum(i, last), 0)))
    outs = pl.pallas_call(
        _pre_kernel,
        grid=(steps,),
        in_specs=[row(d)] + [_resident(w.shape) for w in weights] + slabs_in,
        out_specs=[row(d), row(wm.shape[1]), row(wa.shape[1]), pl.BlockSpec((ng, tm), lambda i: (0, i))]
                  + slabs_out,
        out_shape=[jax.ShapeDtypeStruct((t, d), F32),
                   jax.ShapeDtypeStruct((t, wm.shape[1]), BF16),
                   jax.ShapeDtypeStruct((t, wa.shape[1]), BF16),
                   jax.ShapeDtypeStruct((ng, t), F32)]
                  + [jax.ShapeDtypeStruct(w.shape[1:], BF16) for w in later_weights],
        compiler_params=pltpu.CompilerParams(
            dimension_semantics=("arbitrary",), vmem_limit_bytes=VMEM_LIMIT_BYTES_V7X),
        name="pre_ffn_inproj",
    )(h, *weights, *later_weights)
    return outs[:4], outs[4:]


def _post_kernel(h_ref, ym_ref, ya_ref, p_ref, wout_ref, g2_ref, win_ref, wo_ref,
                 gp_ref, wpg_ref, wpp_ref, gf_ref, o_ref, *, final_norm):
    dm = ym_ref.shape[1]
    h = h_ref[...] + _dot(ym_ref[...], wout_ref[0:dm, :]) + _dot(ya_ref[...], wout_ref[dm:, :])
    xn = _rmsnorm(h, g2_ref[...]).astype(BF16)
    h = h + 0.5 * _swiglu(xn, win_ref, wo_ref)
    gate = jax.nn.sigmoid(_dot(_rmsnorm(h, gp_ref[...]).astype(BF16), wpg_ref[...]))
    h = h + gate * _dot(p_ref[...].astype(BF16), wpp_ref[...])
    if final_norm:
        h = _rmsnorm(h, gf_ref[...])
    o_ref[...] = h


def _post_call(h1, ym, ya, p, wout, g2, win, wo, gp, wpg, wpp, gf, final_norm):
    t, d = h1.shape
    tm = TOKEN_TILE
    row = lambda n: pl.BlockSpec((tm, n), lambda i: (i, 0))
    weights = (wout, g2, win, wo, gp, wpg, wpp, gf)
    return pl.pallas_call(
        functools.partial(_post_kernel, final_norm=final_norm),
        grid=(t // tm,),
        in_specs=[row(d), row(ym.shape[1]), row(ya.shape[1]), row(p.shape[1])]
                 + [_resident(w.shape) for w in weights],
        out_specs=row(d),
        out_shape=jax.ShapeDtypeStruct((t, d), F32),
        compiler_params=pltpu.CompilerParams(
            dimension_semantics=("arbitrary",), vmem_limit_bytes=VMEM_LIMIT_BYTES_V7X),
        name="post_outproj_ffn_ple",
    )(h1, ym, ya, p, *weights)


def _attn_kernel(lam_ref, slope_ref, g_ref, q_ref, k_ref, v_ref, qn_ref, kn_ref, o_ref,
                 tab_s, vt_s, qq_s, kk_s, st0_s, st1_s, m0_s, *, lam_init, tq, kblk):
    s = k_ref.shape[1]
    nq = s // tq
    nkb = s // kblk
    first_row = pl.program_id(1) == 0

    @pl.when(first_row)
    def _():
        r = lax.broadcasted_iota(jnp.int32, tab_s.shape, 1)
        c = lax.broadcasted_iota(jnp.int32, tab_s.shape, 0)
        tab_s[...] = jnp.abs(r - c + (s - tq)).astype(F32) * slope_ref[0, :, 0:1]

    lam_v = lam_ref[...]
    lam = (jnp.exp(jnp.sum(lam_v[0:1] * lam_v[1:2], axis=1, keepdims=True))
           - jnp.exp(jnp.sum(lam_v[2:3] * lam_v[3:4], axis=1, keepdims=True)) + lam_init)

    vt_s[0:A_DV, :] = v_ref[0].astype(F32).T.astype(BF16)
    vt_s[A_DV:, :] = jnp.ones((vt_s.shape[0] - A_DV, s), BF16)
    qq_s[0:s, :] = q_ref[0]
    qq_s[s:, :] = qn_ref[0]
    kk_s[0] = k_ref[0]
    kk_s[1] = kn_ref[0]
    gain = g_ref[0] * (1.0 - lam_init)
    lane = lax.broadcasted_iota(jnp.int32, (tq, 2 * A_DK), 1)
    neg = jnp.full((1, 2 * tq), -jnp.inf, F32)
    acc0 = jnp.zeros((vt_s.shape[0], 2 * tq), F32)

    def stage(u, dst, m_prev, src):
        if isinstance(u, int):
            row, tile = u // nq, u % nq
        else:
            row, tile = jnp.where(u == nq, 1, 0), u & (nq - 1)
        q = qq_s[pl.ds(_aligned(u * tq, tq), tq), :]
        zero = jnp.zeros_like(q)
        qm = jnp.concatenate([jnp.where(lane < A_DK, q, zero), jnp.where(lane >= A_DK, q, zero)], axis=0)
        mx, acc = neg, acc0
        for kb in range(nkb):
            rows = slice(kb * kblk, (kb + 1) * kblk)
            bias = tab_s[pl.ds(_aligned((nq - 1 - tile) * tq + kb * kblk, kblk), kblk), :]
            sc = _dot_nt(kk_s[row, rows, :], qm) - jnp.concatenate([bias, bias], axis=1)
            dst[rows, :] = sc
            mx = jnp.maximum(mx, jnp.max(sc, axis=0, keepdims=True))
            if src is not None:
                e = jnp.exp2(src[rows, :] - m_prev)
                acc = acc + _dot(vt_s[:, rows], e.astype(BF16))
        return mx, acc

    def finish(u, acc):
        r = 1.0 / acc[A_DV:A_DV + 1, :]
        o = acc[:A_DV, :tq] * r[:, :tq] - lam * (acc[:A_DV, tq:] * r[:, tq:])
        on = o * lax.rsqrt(jnp.mean(o * o, axis=0, keepdims=True) + EPS)
        o_ref[0, pl.ds(_aligned(u * tq, tq), tq), :] = (on.T * gain).astype(o_ref.dtype)

    assert nq % 2 == 0 and nq & (nq - 1) == 0

    @pl.when(first_row)
    def _():
        m0_s[...] = stage(0, st0_s, None, None)[0]

    m1, acc_0 = stage(1, st1_s, m0_s[...], st0_s)
    finish(0, acc_0)
    carry = stage(2, st0_s, m1, st1_s)

    def body(j, carry):
        m_even, acc_odd = carry
        finish(2 * j - 1, acc_odd)
        m_odd, acc_even = stage(2 * j + 1, st1_s, m_even, st0_s)
        finish(2 * j, acc_even)
        return stage(2 * j + 2, st0_s, m_odd, st1_s)

    m_next, acc_last = lax.fori_loop(1, nq // 2, body, carry)
    finish(nq - 1, acc_last)
    m0_s[...] = m_next


def _attn_call(za, lam_vec, slopes, g_anorm, lam_init):
    b, s, _ = za.shape
    tq = ATTN_Q_TILE
    blk = 2 * A_DK
    assert blk == A_DV == LANES_V7X
    head_block = lambda off: pl.BlockSpec((1, s, blk), lambda h, i: (i, 0, off + h))
    nxt = lambda i: jnp.minimum(i + 1, b - 1)
    return pl.pallas_call(
        functools.partial(_attn_kernel, lam_init=lam_init, tq=tq, kblk=ATTN_KEY_BLOCK),
        grid=(A_HEADS, b),
        in_specs=[pl.BlockSpec(lam_vec.shape, lambda h, i: (0, 0)),
                  pl.BlockSpec((1, 1, LANES_V7X), lambda h, i: (h, 0, 0)),
                  pl.BlockSpec((1, 1, A_DV), lambda h, i: (h, 0, 0)),
                  head_block(0), head_block(A_HEADS), head_block(2 * A_HEADS),
                  pl.BlockSpec((1, tq, blk), lambda h, i: (nxt(i), 0, h)),
                  pl.BlockSpec((1, s, blk), lambda h, i: (nxt(i), 0, A_HEADS + h))],
        out_specs=pl.BlockSpec((1, s, A_DV), lambda h, i: (i, 0, h)),
        out_shape=jax.ShapeDtypeStruct((b, s, A_HEADS * A_DV), BF16),
        scratch_shapes=[pltpu.VMEM((2 * s - tq, tq), F32),
                        pltpu.VMEM((A_DV + BF16_ROWS_PER_VREG_V7X, s), BF16),
                        pltpu.VMEM((s + tq, blk), BF16),
                        pltpu.VMEM((2, s, blk), BF16),
                        pltpu.VMEM((s, 2 * tq), F32),
                        pltpu.VMEM((s, 2 * tq), F32),
                        pltpu.VMEM((1, 2 * tq), F32)],
        compiler_params=pltpu.CompilerParams(
            dimension_semantics=("arbitrary", "arbitrary"), vmem_limit_bytes=VMEM_LIMIT_BYTES_V7X),
        name="diff_attention",
    )(lam_vec, slopes, g_anorm, za, za, za, za, za)


def _log_sigmoid(x):
    return jnp.minimum(x, 0.0) - jnp.log(1.0 + jnp.exp(-jnp.abs(x)))


def _chunk_scan(x, op, fill, pos_in_chunk, chunk, reverse):
    n = x.shape[1]
    k = 1
    while k < chunk:
        if reverse:
            shifted = pltpu.roll(x, n - k, 1)
            valid = pos_in_chunk < chunk - k
        else:
            shifted = pltpu.roll(x, k, 1)
            valid = pos_in_chunk >= k
        x = op(x, jnp.where(valid, shifted, fill))
        k *= 2
    return x


def _lane_replicated(row):
    return jnp.broadcast_to(row, (LANES_V7X, row.shape[1])).T


def _mlstm_kernel(zm_ref, zgt_ref, convw_ref, bg_ref, gn_ref, o_ref,
                  qx_s, kt_s, ktb_s, brow_s, cmrow_s, crow_s, c_s, m_s, hf_s, hb_s, *, chunk, rblk):
    s = zm_ref.shape[1]
    nc = s // chunk
    nseq = 2 * M_HEADS
    dqk = M_HEADS * M_DK

    cw = convw_ref[...]
    lane_q = lax.broadcasted_iota(jnp.int32, (rblk, 2 * M_DK), 1)
    for j in range(s // rblk):
        r0 = j * rblk
        halo = SUBLANES_V7X
        parts = []
        if j == 0:
            parts.append(jnp.zeros((halo, 2 * dqk), F32))
        else:
            parts.append(zm_ref[0, r0 - halo:r0, 0:2 * dqk].astype(F32))
        parts.append(zm_ref[0, r0:r0 + rblk, 0:2 * dqk].astype(F32))
        if j == s // rblk - 1:
            parts.append(jnp.zeros((halo, 2 * dqk), F32))
        else:
            parts.append(zm_ref[0, r0 + rblk:r0 + rblk + halo, 0:2 * dqk].astype(F32))
        xe = jnp.concatenate(parts, axis=0)
        ne = rblk + 2 * halo
        y = (pltpu.roll(xe, 1, 0) * cw[0:1] + xe * cw[1:2] + pltpu.roll(xe, ne - 1, 0) * cw[2:3])
        y = y[halo:halo + rblk]
        y = y * jax.nn.sigmoid(y)
        q = y[:, :dqk].astype(BF16)
        kk = y[:, dqk:] * (M_DK ** -0.5)
        kt = kk.T
        kt_s[:, r0:r0 + rblk] = kt
        ktb_s[:, r0:r0 + rblk] = kt.astype(BF16)
        for h in range(M_HEADS):
            pair = q[:, (h // 2) * 2 * M_DK:(h // 2 + 1) * 2 * M_DK]
            keep = (lane_q >= M_DK) if (h % 2) else (lane_q < M_DK)
            qx_s[r0:r0 + rblk, h * 2 * M_DK:(h + 1) * 2 * M_DK] = jnp.where(keep, pair, jnp.zeros_like(pair))

    g = zgt_ref[...] + bg_ref[...]
    li = g[0:nseq] * LOG2E
    lf = _log_sigmoid(g[nseq:2 * nseq]) * LOG2E
    pos = lax.broadcasted_iota(jnp.int32, li.shape, 1) % chunk
    is_fw = lax.broadcasted_iota(jnp.int32, li.shape, 0) < M_HEADS
    b = jnp.where(is_fw, _chunk_scan(lf, jnp.add, 0.0, pos, chunk, False),
                  _chunk_scan(lf, jnp.add, 0.0, pos, chunk, True))
    c = li - b
    cmax = jnp.where(is_fw, _chunk_scan(c, jnp.maximum, -jnp.inf, pos, chunk, False),
                     _chunk_scan(c, jnp.maximum, -jnp.inf, pos, chunk, True))
    for j in range(nc):
        cols = slice(j * chunk, (j + 1) * chunk)
        brow_s[j] = b[:, cols]
        cmrow_s[j] = cmax[:, cols]
        crow_s[j] = c[:, cols]

    c_s[...] = jnp.zeros_like(c_s)
    m_s[...] = jnp.zeros_like(m_s)

    rr = lax.broadcasted_iota(jnp.int32, (chunk, chunk), 0)
    cc = lax.broadcasted_iota(jnp.int32, (chunk, chunk), 1)
    causal = (cc <= rr, cc >= rr)
    ones = jnp.ones((chunk, M_DV), BF16)

    def step(j, carry):
        for d in range(2):
            cj = j if d == 0 else nc - 1 - j
            rows = pl.ds(pl.multiple_of(cj * chunk, chunk), chunk)
            last = chunk - 1 if d == 0 else 0
            h_out = hf_s if d == 0 else hb_s
            for pr in range(M_HEADS // 2):
                state = c_s[d, pr]
                state_bf = state.astype(BF16)
                ktp = ktb_s[pr * 2 * M_DK:(pr + 1) * 2 * M_DK, rows]
                for half in range(2):
                    h = 2 * pr + half
                    i = d * M_HEADS + h
                    blk = slice(h * M_DV, (h + 1) * M_DV)
                    qc = qx_s[rows, blk]
                    v_aug = jnp.concatenate(
                        [zm_ref[0, rows, 2 * dqk + h * M_DV:2 * dqk + (h + 1) * M_DV], ones], axis=1)
                    crow = crow_s[cj, i:i + 1, :]
                    m_old = m_s[i:i + 1, :]
                    b_rep = _lane_replicated(brow_s[cj, i:i + 1, :])
                    m_rep = jnp.maximum(m_old, _lane_replicated(cmrow_s[cj, i:i + 1, :]))
                    dt = jnp.where(causal[d], jnp.exp2(crow - m_rep), 0.0)
                    sm = (_dot(qc, ktp) * dt).astype(BF16)
                    qw = (qc * jnp.exp2(m_old - m_rep)).astype(BF16)
                    tot = _dot(jnp.concatenate([qw, sm], axis=1),
                               jnp.concatenate([state_bf, v_aug], axis=0))
                    floor = jnp.exp2(-(b_rep + m_rep))
                    h_out[rows, blk] = tot[:, :M_DV] / jnp.maximum(jnp.abs(tot[:, M_DV:]), floor)
                    m_last = m_rep[last:last + 1, :]
                    wk = jnp.exp2(crow - m_last[:, 0:1])
                    kth = kt_s[h * M_DK:(h + 1) * M_DK, rows]
                    dc = _dot((kth * wk).astype(BF16), v_aug)
                    sub = slice(half * M_DK, (half + 1) * M_DK)
                    c_s[d, pr, sub, :] = jnp.exp2(m_old - m_last)[:, 0:1] * state[sub] + dc
                    m_s[i:i + 1, :] = b_rep[last:last + 1, :] + m_last
        return carry

    def steps(jj, carry):
        for k in range(MLSTM_CHUNKS_PER_STEP):
            step(jj * MLSTM_CHUNKS_PER_STEP + k, carry)
        return carry

    lax.fori_loop(0, nc // MLSTM_CHUNKS_PER_STEP, steps, 0)

    gn = gn_ref[...]
    dv = M_HEADS * M_DV

    def finish(j, carry):
        rows = pl.ds(pl.multiple_of(j * rblk, rblk), rblk)
        for h in range(M_HEADS):
            blk = slice(h * M_DV, (h + 1) * M_DV)
            hh = hf_s[rows, blk] + hb_s[rows, blk]
            hn = hh * lax.rsqrt(jnp.mean(hh * hh, axis=-1, keepdims=True) + EPS) * gn[:, blk]
            og = zm_ref[0, rows, 2 * dqk + dv + h * M_DV:2 * dqk + dv + (h + 1) * M_DV].astype(F32)
            o_ref[0, rows, blk] = (jax.nn.sigmoid(og) * hn).astype(o_ref.dtype)
        return carry

    lax.fori_loop(0, s // rblk, finish, 0)


def _mlstm_call(zm, zgt, conv_w, bg, g_mnorm):
    b, s, wz = zm.shape
    chunk = MLSTM_CHUNK
    nc = s // chunk
    dv = M_HEADS * M_DV
    dqk = M_HEADS * M_DK
    nseq = 2 * M_HEADS
    full = lambda a: pl.BlockSpec(a.shape, lambda i: (0,) * a.ndim)
    per_b = lambda n: pl.BlockSpec((1, s, n), lambda i: (i, 0, 0))
    gate_rows = lambda: pltpu.VMEM((nc, nseq, chunk), F32)
    return pl.pallas_call(
        functools.partial(_mlstm_kernel, chunk=chunk, rblk=MLSTM_ROW_BLOCK),
        grid=(b,),
        in_specs=[per_b(wz), pl.BlockSpec((zgt.shape[0], s), lambda i: (0, i)), full(conv_w), full(bg),
                  full(g_mnorm)],
        out_specs=per_b(dv),
        out_shape=jax.ShapeDtypeStruct((b, s, dv), BF16),
        scratch_shapes=[
            pltpu.VMEM((s, M_HEADS * 2 * M_DK), BF16),
            pltpu.VMEM((dqk, s), F32),
            pltpu.VMEM((dqk, s), BF16),
            gate_rows(),
            gate_rows(),
            gate_rows(),
            pltpu.VMEM((2, M_HEADS // 2, 2 * M_DK, 2 * M_DV), F32),
            pltpu.VMEM((nseq, LANES_V7X), F32),
            pltpu.VMEM((s, dv), F32),
            pltpu.VMEM((s, dv), F32),
        ],
        compiler_params=pltpu.CompilerParams(
            dimension_semantics=("arbitrary",), vmem_limit_bytes=VMEM_LIMIT_BYTES_V7X),
        name="mlstm_bidir",
    )(zm, zgt, conv_w, bg, g_mnorm)


def _gate_rows(a16):
    hh = M_HEADS
    return jnp.concatenate([a16[..., 0:hh], a16[..., 2 * hh:3 * hh], a16[..., hh:2 * hh], a16[..., 3 * hh:]],
                           axis=-1)


def kernel(x, p, g_ffn1, w_ffn1_in, w_ffn1_out, g_mix, w_in, b_mgate, conv_w, g_mnorm, lam_q1, lam_k1, lam_q2, lam_k2, g_anorm, w_out, g_ffn2, w_ffn2_in, w_ffn2_out, g_ple, w_ple_gate, w_ple_proj, g_final):
    bsz, s, d = x.shape
    depth = w_in.shape[0]
    t = bsz * s
    m_qk, m_v, m_g = M_HEADS * M_DK, M_HEADS * M_DV, 4 * M_HEADS
    a_qk = A_HEADS * 2 * A_DK
    m_end = 2 * m_qk + 2 * m_v
    slopes = np.array([2.0 ** (-8.0 * (h + 1) / A_HEADS) for h in range(A_HEADS)], dtype=np.float32)
    slopes = jnp.broadcast_to(jnp.asarray(slopes * np.float32(LOG2E))[:, None, None],
                              (A_HEADS, 1, LANES_V7X))
    row = lambda v: v.reshape(1, -1).astype(F32)

    h = x.reshape(t, d)
    for i in range(depth):
        lam_init = 0.8 - 0.6 * math.exp(-0.3 * i)
        wi = w_in[i]
        wm = wi[:, :m_end].astype(BF16)
        wa = wi[:, m_end + m_g:]
        wa = jnp.concatenate([wa[:, :a_qk] * (A_DK ** -0.5 * LOG2E), wa[:, a_qk:]], axis=1).astype(BF16)
        wgt = _gate_rows(wi[:, m_end:m_end + m_g]).T.astype(BF16)
        bg = _gate_rows(b_mgate[i].astype(F32)).reshape(m_g, 1)

        (h1, zm, za, zgt), (win2, wo2, wout, wpg, wpp) = _pre_call(
            h, row(g_ffn1[i]), w_ffn1_in[i].astype(BF16), w_ffn1_out[i].astype(BF16), row(g_mix[i]), wm, wa, wgt,
            i, (w_ffn2_in, w_ffn2_out, w_out, w_ple_gate, w_ple_proj))
        ym = _mlstm_call(zm.reshape(bsz, s, -1), zgt, conv_w[i].astype(F32), bg, row(g_mnorm[i]))
        lam_vec = jnp.stack([lam_q1[i], lam_k1[i], lam_q2[i], lam_k2[i]]).astype(F32)
        ya = _attn_call(za.reshape(bsz, s, -1), lam_vec, slopes,
                        g_anorm[i].reshape(A_HEADS, 1, A_DV).astype(F32), lam_init)
        h = _post_call(h1, ym.reshape(t, -1), ya.reshape(t, -1), p[i].reshape(t, -1),
                       wout, row(g_ffn2[i]), win2, wo2, row(g_ple[i]), wpg, wpp, row(g_final),
                       final_norm=(i == depth - 1))
    return h.reshape(bsz, s, d)
```

```python
import functools
import math

import jax
import jax.numpy as jnp
import numpy as np
from jax import lax
from jax.experimental import pallas as pl
from jax.experimental.pallas import tpu as pltpu

F32 = jnp.float32
BF16 = jnp.bfloat16

EPS = 1e-6
LOG2E = 1.4426950408889634

M_HEADS = 4
M_DK = 64
M_DV = 128
A_HEADS = 4
A_DK = 64
A_DV = 128
CONV_K = 3

LANES_V7X = 128
SUBLANES_V7X = 8
BF16_ROWS_PER_VREG_V7X = 16
VMEM_LIMIT_BYTES_V7X = 56 * 1024 * 1024

TOKEN_TILE = 512
FF_CHUNK = 256
MLSTM_CHUNK = 128
MLSTM_CHUNKS_PER_STEP = 4
MLSTM_ROW_BLOCK = 256
ATTN_Q_TILE = 256
ATTN_KEY_BLOCK = 256


def _aligned(x, m):
    return x if isinstance(x, int) else pl.multiple_of(x, m)


def _dot(a, b):
    return jnp.dot(a, b, preferred_element_type=F32)


def _dot_nt(a, b):
    return lax.dot_general(a, b, (((1,), (1,)), ((), ())), preferred_element_type=F32)


def _rmsnorm(x, g):
    ms = jnp.mean(x * x, axis=-1, keepdims=True)
    return x * lax.rsqrt(ms + EPS) * g


def _swiglu(xn, win_ref, wo_ref):
    f = wo_ref.shape[0]
    parts = []
    for c0 in range(0, f, FF_CHUNK):
        g = _dot(xn, win_ref[:, c0:c0 + FF_CHUNK])
        u = _dot(xn, win_ref[:, f + c0:f + c0 + FF_CHUNK])
        parts.append((g * jax.nn.sigmoid(g) * u).astype(BF16))
    return _dot(jnp.concatenate(parts, axis=1), wo_ref[...])


def _resident(shape):
    nd = len(shape)
    return pl.BlockSpec(shape, lambda *_: (0,) * nd, pipeline_mode=pl.Buffered(1))


def _pre_kernel(x_ref, g1_ref, win_ref, wo_ref, gm_ref, wm_ref, wa_ref, wgt_ref, *rest):
    n_cast = (len(rest) - 4) // 2
    cast_in, (h_ref, zm_ref, za_ref, zgt_ref), cast_out = rest[:n_cast], rest[n_cast:n_cast + 4], rest[n_cast + 4:]
    x = x_ref[...]
    xn = _rmsnorm(x, g1_ref[...]).astype(BF16)
    h = x + 0.5 * _swiglu(xn, win_ref, wo_ref)
    h_ref[...] = h
    u = _rmsnorm(h, gm_ref[...]).astype(BF16)
    zm_ref[...] = _dot(u, wm_ref[...]).astype(BF16)
    za_ref[...] = _dot(u, wa_ref[...]).astype(BF16)
    zgt_ref[...] = _dot_nt(wgt_ref[...], u)
    for src, dst in zip(cast_in, cast_out):
        dst[...] = src[0].astype(BF16)


def _slab_rows(rows, steps):
    for r in range(BF16_ROWS_PER_VREG_V7X, rows + 1, BF16_ROWS_PER_VREG_V7X):
        if rows % r == 0 and rows // r <= steps:
            return r
    raise ValueError((rows, steps))


def _pre_call(h, g1, win, wo, gm, wm, wa, wgt, layer, later_weights):
    t, d = h.shape
    tm = TOKEN_TILE
    ng = wgt.shape[0]
    steps = t // tm
    row = lambda n: pl.BlockSpec((tm, n), lambda i: (i, 0))
    weights = (g1, win, wo, gm, wm, wa, wgt)
    slabs_in, slabs_out = [], []
    for w in later_weights:
        r = _slab_rows(w.shape[1], steps)
        last = w.shape[1] // r - 1
        slabs_in.append(pl.BlockSpec((1, r, w.shape[2]), lambda i, last=last: (layer, jnp.minimum(i, last), 0)))
        slabs_out.append(pl.BlockSpec((r, w.shape[2]), lambda i, last=last: (jnp.minimum(i, last), 0)))
    outs = pl.pallas_call(
        _pre_kernel,
        grid=(steps,),
        in_specs=[row(d)] + [_resident(w.shape) for w in weights] + slabs_in,
        out_specs=[row(d), row(wm.shape[1]), row(wa.shape[1]), pl.BlockSpec((ng, tm), lambda i: (0, i))]
                  + slabs_out,
        out_shape=[jax.ShapeDtypeStruct((t, d), F32),
                   jax.ShapeDtypeStruct((t, wm.shape[1]), BF16),
                   jax.ShapeDtypeStruct((t, wa.shape[1]), BF16),
                   jax.ShapeDtypeStruct((ng, t), F32)]
                  + [jax.ShapeDtypeStruct(w.shape[1:], BF16) for w in later_weights],
        compiler_params=pltpu.CompilerParams(
            dimension_semantics=("arbitrary",), vmem_limit_bytes=VMEM_LIMIT_BYTES_V7X),
        name="pre_ffn_inproj",
    )(h, *weights, *later_weights)
    return outs[:4], outs[4:]


def _post_kernel(h_ref, hf_ref, hb_ref, og_ref, ya_ref, p_ref, gn_ref, wout_ref, g2_ref, win_ref, wo_ref,
                 gp_ref, wpg_ref, wpp_ref, gf_ref, o_ref, *, final_norm):
    dm = hf_ref.shape[1]
    h = h_ref[...] + _dot(ya_ref[...], wout_ref[dm:, :])
    heads = []
    for c0 in range(0, dm, M_DV):
        hh = hf_ref[:, c0:c0 + M_DV] + hb_ref[:, c0:c0 + M_DV]
        hn = hh * lax.rsqrt(jnp.mean(hh * hh, axis=-1, keepdims=True) + EPS) * gn_ref[:, c0:c0 + M_DV]
        heads.append((jax.nn.sigmoid(og_ref[:, c0:c0 + M_DV].astype(F32)) * hn).astype(BF16))
    h = h + _dot(jnp.concatenate(heads, axis=1), wout_ref[0:dm, :])
    xn = _rmsnorm(h, g2_ref[...]).astype(BF16)
    h = h + 0.5 * _swiglu(xn, win_ref, wo_ref)
    gate = jax.nn.sigmoid(_dot(_rmsnorm(h, gp_ref[...]).astype(BF16), wpg_ref[...]))
    h = h + gate * _dot(p_ref[...].astype(BF16), wpp_ref[...])
    if final_norm:
        h = _rmsnorm(h, gf_ref[...])
    o_ref[...] = h


def _post_call(h1, hf, hb, zm, og_block, ya, p, gn, wout, g2, win, wo, gp, wpg, wpp, gf, final_norm):
    t, d = h1.shape
    tm = TOKEN_TILE
    row = lambda n: pl.BlockSpec((tm, n), lambda i: (i, 0))
    weights = (gn, wout, g2, win, wo, gp, wpg, wpp, gf)
    return pl.pallas_call(
        functools.partial(_post_kernel, final_norm=final_norm),
        grid=(t // tm,),
        in_specs=[row(d), row(hf.shape[1]), row(hb.shape[1]),
                  pl.BlockSpec((tm, hf.shape[1]), lambda i: (i, og_block)), row(ya.shape[1]), row(p.shape[1])]
                 + [_resident(w.shape) for w in weights],
        out_specs=row(d),
        out_shape=jax.ShapeDtypeStruct((t, d), F32),
        compiler_params=pltpu.CompilerParams(
            dimension_semantics=("arbitrary",), vmem_limit_bytes=VMEM_LIMIT_BYTES_V7X),
        name="post_outproj_ffn_ple",
    )(h1, hf, hb, zm, ya, p, *weights)


def _attn_kernel(lam_ref, slope_ref, g_ref, q_ref, k_ref, v_ref, qn_ref, kn_ref, o_ref,
                 tab_s, vt_s, qq_s, kk_s, st0_s, st1_s, m0_s, *, lam_init, tq, kblk):
    s = k_ref.shape[1]
    nq = s // tq
    nkb = s // kblk
    first_row = pl.program_id(1) == 0

    @pl.when(first_row)
    def _():
        r = lax.broadcasted_iota(jnp.int32, tab_s.shape, 1)
        c = lax.broadcasted_iota(jnp.int32, tab_s.shape, 0)
        tab_s[...] = jnp.abs(r - c + (s - tq)).astype(F32) * slope_ref[0, :, 0:1]

    lam_v = lam_ref[...]
    lam = (jnp.exp(jnp.sum(lam_v[0:1] * lam_v[1:2], axis=1, keepdims=True))
           - jnp.exp(jnp.sum(lam_v[2:3] * lam_v[3:4], axis=1, keepdims=True)) + lam_init)

    vt_s[0:A_DV, :] = v_ref[0].astype(F32).T.astype(BF16)
    vt_s[A_DV:, :] = jnp.ones((vt_s.shape[0] - A_DV, s), BF16)
    qq_s[0:s, :] = q_ref[0]
    qq_s[s:, :] = qn_ref[0]
    kk_s[0] = k_ref[0]
    kk_s[1] = kn_ref[0]
    gain = g_ref[0] * (1.0 - lam_init)
    lane = lax.broadcasted_iota(jnp.int32, (tq, 2 * A_DK), 1)
    neg = jnp.full((1, 2 * tq), -jnp.inf, F32)
    acc0 = jnp.zeros((vt_s.shape[0], 2 * tq), F32)

    def stage(u, dst, m_prev, src):
        if isinstance(u, int):
            row, tile = u // nq, u % nq
        else:
            row, tile = jnp.where(u == nq, 1, 0), u & (nq - 1)
        q = qq_s[pl.ds(_aligned(u * tq, tq), tq), :]
        zero = jnp.zeros_like(q)
        qm = jnp.concatenate([jnp.where(lane < A_DK, q, zero), jnp.where(lane >= A_DK, q, zero)], axis=0)
        mx, acc = neg, acc0
        for kb in range(nkb):
            rows = slice(kb * kblk, (kb + 1) * kblk)
            bias = tab_s[pl.ds(_aligned((nq - 1 - tile) * tq + kb * kblk, kblk), kblk), :]
            sc = _dot_nt(kk_s[row, rows, :], qm) - jnp.concatenate([bias, bias], axis=1)
            dst[rows, :] = sc
            mx = jnp.maximum(mx, jnp.max(sc, axis=0, keepdims=True))
            if src is not None:
                e = jnp.exp2(src[rows, :] - m_prev)
                acc = acc + _dot(vt_s[:, rows], e.astype(BF16))
        return mx, acc

    def finish(u, acc):
        r = 1.0 / acc[A_DV:A_DV + 1, :]
        o = acc[:A_DV, :tq] * r[:, :tq] - lam * (acc[:A_DV, tq:] * r[:, tq:])
        on = o * lax.rsqrt(jnp.mean(o * o, axis=0, keepdims=True) + EPS)
        o_ref[0, pl.ds(_aligned(u * tq, tq), tq), :] = (on.T * gain).astype(o_ref.dtype)

    assert nq % 2 == 0 and nq & (nq - 1) == 0

    @pl.when(first_row)
    def _():
        m0_s[...] = stage(0, st0_s, None, None)[0]

    m1, acc_0 = stage(1, st1_s, m0_s[...], st0_s)
    finish(0, acc_0)
    carry = stage(2, st0_s, m1, st1_s)

    def body(j, carry):
        m_even, acc_odd = carry
        finish(2 * j - 1, acc_odd)
        m_odd, acc_even = stage(2 * j + 1, st1_s, m_even, st0_s)
        finish(2 * j, acc_even)
        return stage(2 * j + 2, st0_s, m_odd, st1_s)

    m_next, acc_last = lax.fori_loop(1, nq // 2, body, carry)
    finish(nq - 1, acc_last)
    m0_s[...] = m_next


def _attn_call(za, lam_vec, slopes, g_anorm, lam_init):
    b, s, _ = za.shape
    tq = ATTN_Q_TILE
    blk = 2 * A_DK
    assert blk == A_DV == LANES_V7X
    head_block = lambda off: pl.BlockSpec((1, s, blk), lambda h, i: (i, 0, off + h))
    nxt = lambda i: jnp.minimum(i + 1, b - 1)
    return pl.pallas_call(
        functools.partial(_attn_kernel, lam_init=lam_init, tq=tq, kblk=ATTN_KEY_BLOCK),
        grid=(A_HEADS, b),
        in_specs=[pl.BlockSpec(lam_vec.shape, lambda h, i: (0, 0)),
                  pl.BlockSpec((1, 1, LANES_V7X), lambda h, i: (h, 0, 0)),
                  pl.BlockSpec((1, 1, A_DV), lambda h, i: (h, 0, 0)),
                  head_block(0), head_block(A_HEADS), head_block(2 * A_HEADS),
                  pl.BlockSpec((1, tq, blk), lambda h, i: (nxt(i), 0, h)),
                  pl.BlockSpec((1, s, blk), lambda h, i: (nxt(i), 0, A_HEADS + h))],
        out_specs=pl.BlockSpec((1, s, A_DV), lambda h, i: (i, 0, h)),
        out_shape=jax.ShapeDtypeStruct((b, s, A_HEADS * A_DV), BF16),
        scratch_shapes=[pltpu.VMEM((2 * s - tq, tq), F32),
                        pltpu.VMEM((A_DV + BF16_ROWS_PER_VREG_V7X, s), BF16),
                        pltpu.VMEM((s + tq, blk), BF16),
                        pltpu.VMEM((2, s, blk), BF16),
                        pltpu.VMEM((s, 2 * tq), F32),
                        pltpu.VMEM((s, 2 * tq), F32),
                        pltpu.VMEM((1, 2 * tq), F32)],
        compiler_params=pltpu.CompilerParams(
            dimension_semantics=("arbitrary", "arbitrary"), vmem_limit_bytes=VMEM_LIMIT_BYTES_V7X),
        name="diff_attention",
    )(lam_vec, slopes, g_anorm, za, za, za, za, za)


def _log_sigmoid(x):
    return jnp.minimum(x, 0.0) - jnp.log(1.0 + jnp.exp(-jnp.abs(x)))


def _chunk_scan(x, op, fill, pos_in_chunk, chunk, reverse):
    n = x.shape[1]
    k = 1
    while k < chunk:
        if reverse:
            shifted = pltpu.roll(x, n - k, 1)
            valid = pos_in_chunk < chunk - k
        else:
            shifted = pltpu.roll(x, k, 1)
            valid = pos_in_chunk >= k
        x = op(x, jnp.where(valid, shifted, fill))
        k *= 2
    return x


def _lane_replicated(row):
    return jnp.broadcast_to(row, (LANES_V7X, row.shape[1])).T


def _mlstm_kernel(zm_ref, zgt_ref, convw_ref, bg_ref, hf_ref, hb_ref,
                  qx_s, kt_s, ktb_s, brow_s, cmrow_s, crow_s, c_s, m_s, *, chunk, rblk):
    s = zm_ref.shape[1]
    nc = s // chunk
    nseq = 2 * M_HEADS
    dqk = M_HEADS * M_DK

    cw = convw_ref[...]
    lane_q = lax.broadcasted_iota(jnp.int32, (rblk, 2 * M_DK), 1)
    for j in range(s // rblk):
        r0 = j * rblk
        halo = SUBLANES_V7X
        parts = []
        if j == 0:
            parts.append(jnp.zeros((halo, 2 * dqk), F32))
        else:
            parts.append(zm_ref[0, r0 - halo:r0, 0:2 * dqk].astype(F32))
        parts.append(zm_ref[0, r0:r0 + rblk, 0:2 * dqk].astype(F32))
        if j == s // rblk - 1:
            parts.append(jnp.zeros((halo, 2 * dqk), F32))
        else:
            parts.append(zm_ref[0, r0 + rblk:r0 + rblk + halo, 0:2 * dqk].astype(F32))
        xe = jnp.concatenate(parts, axis=0)
        ne = rblk + 2 * halo
        y = (pltpu.roll(xe, 1, 0) * cw[0:1] + xe * cw[1:2] + pltpu.roll(xe, ne - 1, 0) * cw[2:3])
        y = y[halo:halo + rblk]
        y = y * jax.nn.sigmoid(y)
        q = y[:, :dqk].astype(BF16)
        kk = y[:, dqk:] * (M_DK ** -0.5)
        kt = kk.T
        kt_s[:, r0:r0 + rblk] = kt
        ktb_s[:, r0:r0 + rblk] = kt.astype(BF16)
        for h in range(M_HEADS):
            pair = q[:, (h // 2) * 2 * M_DK:(h // 2 + 1) * 2 * M_DK]
            keep = (lane_q >= M_DK) if (h % 2) else (lane_q < M_DK)
            qx_s[r0:r0 + rblk, h * 2 * M_DK:(h + 1) * 2 * M_DK] = jnp.where(keep, pair, jnp.zeros_like(pair))

    g = zgt_ref[...] + bg_ref[...]
    li = g[0:nseq] * LOG2E
    lf = _log_sigmoid(g[nseq:2 * nseq]) * LOG2E
    pos = lax.broadcasted_iota(jnp.int32, li.shape, 1) % chunk
    is_fw = lax.broadcasted_iota(jnp.int32, li.shape, 0) < M_HEADS
    b = jnp.where(is_fw, _chunk_scan(lf, jnp.add, 0.0, pos, chunk, False),
                  _chunk_scan(lf, jnp.add, 0.0, pos, chunk, True))
    c = li - b
    cmax = jnp.where(is_fw, _chunk_scan(c, jnp.maximum, -jnp.inf, pos, chunk, False),
                     _chunk_scan(c, jnp.maximum, -jnp.inf, pos, chunk, True))
    for j in range(nc):
        cols = slice(j * chunk, (j + 1) * chunk)
        brow_s[j] = b[:, cols]
        cmrow_s[j] = cmax[:, cols]
        crow_s[j] = c[:, cols]

    c_s[...] = jnp.zeros_like(c_s)
    m_s[...] = jnp.zeros_like(m_s)

    rr = lax.broadcasted_iota(jnp.int32, (chunk, chunk), 0)
    cc = lax.broadcasted_iota(jnp.int32, (chunk, chunk), 1)
    causal = (cc <= rr, cc >= rr)
    ones = jnp.ones((chunk, M_DV), BF16)

    def step(j, carry):
        for d in range(2):
            cj = j if d == 0 else nc - 1 - j
            rows = pl.ds(pl.multiple_of(cj * chunk, chunk), chunk)
            last = chunk - 1 if d == 0 else 0
            h_out = hf_ref if d == 0 else hb_ref
            for pr in range(M_HEADS // 2):
                state = c_s[d, pr]
                state_bf = state.astype(BF16)
                ktp = ktb_s[pr * 2 * M_DK:(pr + 1) * 2 * M_DK, rows]
                for half in range(2):
                    h = 2 * pr + half
                    i = d * M_HEADS + h
                    blk = slice(h * M_DV, (h + 1) * M_DV)
                    qc = qx_s[rows, blk]
                    v_aug = jnp.concatenate(
                        [zm_ref[0, rows, 2 * dqk + h * M_DV:2 * dqk + (h + 1) * M_DV], ones], axis=1)
                    crow = crow_s[cj, i:i + 1, :]
                    m_old = m_s[i:i + 1, :]
                    b_rep = _lane_replicated(brow_s[cj, i:i + 1, :])
                    m_rep = jnp.maximum(m_old, _lane_replicated(cmrow_s[cj, i:i + 1, :]))
                    dt = jnp.where(causal[d], jnp.exp2(crow - m_rep), 0.0)
                    sm = (_dot(qc, ktp) * dt).astype(BF16)
                    qw = (qc * jnp.exp2(m_old - m_rep)).astype(BF16)
                    tot = _dot(jnp.concatenate([qw, sm], axis=1),
                               jnp.concatenate([state_bf, v_aug], axis=0))
                    floor = jnp.exp2(-(b_rep + m_rep))
                    h_out[0, rows, blk] = tot[:, :M_DV] / jnp.maximum(jnp.abs(tot[:, M_DV:]), floor)
                    m_last = m_rep[last:last + 1, :]
                    wk = jnp.exp2(crow - m_last[:, 0:1])
                    kth = kt_s[h * M_DK:(h + 1) * M_DK, rows]
                    dc = _dot((kth * wk).astype(BF16), v_aug)
                    sub = slice(half * M_DK, (half + 1) * M_DK)
                    c_s[d, pr, sub, :] = jnp.exp2(m_old - m_last)[:, 0:1] * state[sub] + dc
                    m_s[i:i + 1, :] = b_rep[last:last + 1, :] + m_last
        return carry

    def steps(jj, carry):
        for k in range(MLSTM_CHUNKS_PER_STEP):
            step(jj * MLSTM_CHUNKS_PER_STEP + k, carry)
        return carry

    lax.fori_loop(0, nc // MLSTM_CHUNKS_PER_STEP, steps, 0)


def _mlstm_call(zm, zgt, conv_w, bg):
    b, s, wz = zm.shape
    chunk = MLSTM_CHUNK
    nc = s // chunk
    dv = M_HEADS * M_DV
    dqk = M_HEADS * M_DK
    nseq = 2 * M_HEADS
    full = lambda a: pl.BlockSpec(a.shape, lambda i: (0,) * a.ndim)
    per_b = lambda n: pl.BlockSpec((1, s, n), lambda i: (i, 0, 0))
    gate_rows = lambda: pltpu.VMEM((nc, nseq, chunk), F32)
    return pl.pallas_call(
        functools.partial(_mlstm_kernel, chunk=chunk, rblk=MLSTM_ROW_BLOCK),
        grid=(b,),
        in_specs=[per_b(wz), pl.BlockSpec((zgt.shape[0], s), lambda i: (0, i)), full(conv_w), full(bg)],
        out_specs=[per_b(dv), per_b(dv)],
        out_shape=[jax.ShapeDtypeStruct((b, s, dv), F32), jax.ShapeDtypeStruct((b, s, dv), F32)],
        scratch_shapes=[
            pltpu.VMEM((s, M_HEADS * 2 * M_DK), BF16),
            pltpu.VMEM((dqk, s), F32),
            pltpu.VMEM((dqk, s), BF16),
            gate_rows(),
            gate_rows(),
            gate_rows(),
            pltpu.VMEM((2, M_HEADS // 2, 2 * M_DK, 2 * M_DV), F32),
            pltpu.VMEM((nseq, LANES_V7X), F32),
        ],
        compiler_params=pltpu.CompilerParams(
            dimension_semantics=("arbitrary",), vmem_limit_bytes=VMEM_LIMIT_BYTES_V7X),
        name="mlstm_bidir",
    )(zm, zgt, conv_w, bg)


def _gate_rows(a16):
    hh = M_HEADS
    return jnp.concatenate([a16[..., 0:hh], a16[..., 2 * hh:3 * hh], a16[..., hh:2 * hh], a16[..., 3 * hh:]],
                           axis=-1)


def kernel(x, p, g_ffn1, w_ffn1_in, w_ffn1_out, g_mix, w_in, b_mgate, conv_w, g_mnorm, lam_q1, lam_k1, lam_q2, lam_k2, g_anorm, w_out, g_ffn2, w_ffn2_in, w_ffn2_out, g_ple, w_ple_gate, w_ple_proj, g_final):
    bsz, s, d = x.shape
    depth = w_in.shape[0]
    t = bsz * s
    m_qk, m_v, m_g = M_HEADS * M_DK, M_HEADS * M_DV, 4 * M_HEADS
    a_qk = A_HEADS * 2 * A_DK
    m_end = 2 * m_qk + 2 * m_v
    slopes = np.array([2.0 ** (-8.0 * (h + 1) / A_HEADS) for h in range(A_HEADS)], dtype=np.float32)
    slopes = jnp.broadcast_to(jnp.asarray(slopes * np.float32(LOG2E))[:, None, None],
                              (A_HEADS, 1, LANES_V7X))
    row = lambda v: v.reshape(1, -1).astype(F32)

    h = x.reshape(t, d)
    for i in range(depth):
        lam_init = 0.8 - 0.6 * math.exp(-0.3 * i)
        wi = w_in[i]
        wm = wi[:, :m_end].astype(BF16)
        wa = wi[:, m_end + m_g:]
        wa = jnp.concatenate([wa[:, :a_qk] * (A_DK ** -0.5 * LOG2E), wa[:, a_qk:]], axis=1).astype(BF16)
        wgt = _gate_rows(wi[:, m_end:m_end + m_g]).T.astype(BF16)
        bg = _gate_rows(b_mgate[i].astype(F32)).reshape(m_g, 1)

        (h1, zm, za, zgt), (win2, wo2, wout, wpg, wpp) = _pre_call(
            h, row(g_ffn1[i]), w_ffn1_in[i].astype(BF16), w_ffn1_out[i].astype(BF16), row(g_mix[i]), wm, wa, wgt,
            i, (w_ffn2_in, w_ffn2_out, w_out, w_ple_gate, w_ple_proj))
        hf, hb = _mlstm_call(zm.reshape(bsz, s, -1), zgt, conv_w[i].astype(F32), bg)
        lam_vec = jnp.stack([lam_q1[i], lam_k1[i], lam_q2[i], lam_k2[i]]).astype(F32)
        ya = _attn_call(za.reshape(bsz, s, -1), lam_vec, slopes,
                        g_anorm[i].reshape(A_HEADS, 1, A_DV).astype(F32), lam_init)
        assert (2 * m_qk + m_v) % m_v == 0
        h = _post_call(h1, hf.reshape(t, -1), hb.reshape(t, -1), zm, (2 * m_qk + m_v) // m_v, ya.reshape(t, -1),
                       p[i].reshape(t, -1), row(g_mnorm[i]),
                       wout, row(g_ffn2[i]), win2, wo2, row(g_ple[i]), wpg, wpp, row(g_final),
                       final_norm=(i == depth - 1))
    return h.reshape(bsz, s, d)
```

```python
import functools
import math

import jax
import jax.numpy as jnp
import numpy as np
from jax import lax
from jax.experimental import pallas as pl
from jax.experimental.pallas import tpu as pltpu

F32 = jnp.float32
BF16 = jnp.bfloat16

EPS = 1e-6
LOG2E = 1.4426950408889634

M_HEADS = 4
M_DK = 64
M_DV = 128
A_HEADS = 4
A_DK = 64
A_DV = 128
CONV_K = 3

LANES_V7X = 128
SUBLANES_V7X = 8
BF16_ROWS_PER_VREG_V7X = 16
VMEM_LIMIT_BYTES_V7X = 56 * 1024 * 1024

TOKEN_TILE = 512
FF_CHUNK = 256
MLSTM_CHUNK = 128
MLSTM_CHUNKS_PER_STEP = 8
MLSTM_ROW_BLOCK = 128
ATTN_Q_TILE = 256
ATTN_KEY_BLOCK = 256


def _aligned(x, m):
    return x if isinstance(x, int) else pl.multiple_of(x, m)


def _dot(a, b):
    return jnp.dot(a, b, preferred_element_type=F32)


def _dot_nt(a, b):
    return lax.dot_general(a, b, (((1,), (1,)), ((), ())), preferred_element_type=F32)


def _rmsnorm(x, g):
    ms = jnp.mean(x * x, axis=-1, keepdims=True)
    return x * lax.rsqrt(ms + EPS) * g


def _swiglu(xn, win_ref, wo_ref):
    f = wo_ref.shape[0]
    parts = []
    for c0 in range(0, f, FF_CHUNK):
        g = _dot(xn, win_ref[:, c0:c0 + FF_CHUNK])
        u = _dot(xn, win_ref[:, f + c0:f + c0 + FF_CHUNK])
        parts.append((g * jax.nn.sigmoid(g) * u).astype(BF16))
    return _dot(jnp.concatenate(parts, axis=1), wo_ref[...])


def _resident(shape):
    nd = len(shape)
    return pl.BlockSpec(shape, lambda *_: (0,) * nd, pipeline_mode=pl.Buffered(1))


def _pre_kernel(x_ref, g1_ref, win_ref, wo_ref, gm_ref, wm_ref, wa_ref, wgt_ref, *rest):
    n_cast = (len(rest) - 4) // 2
    cast_in, (h_ref, zm_ref, za_ref, zgt_ref), cast_out = rest[:n_cast], rest[n_cast:n_cast + 4], rest[n_cast + 4:]
    x = x_ref[...]
    xn = _rmsnorm(x, g1_ref[...]).astype(BF16)
    h = x + 0.5 * _swiglu(xn, win_ref, wo_ref)
    h_ref[...] = h
    u = _rmsnorm(h, gm_ref[...]).astype(BF16)
    zm_ref[...] = _dot(u, wm_ref[...]).astype(BF16)
    za_ref[...] = _dot(u, wa_ref[...]).astype(BF16)
    zgt_ref[...] = _dot_nt(wgt_ref[...], u)
    for src, dst in zip(cast_in, cast_out):
        dst[...] = src[0].astype(BF16)


def _slab_rows(rows, steps):
    for r in range(BF16_ROWS_PER_VREG_V7X, rows + 1, BF16_ROWS_PER_VREG_V7X):
        if rows % r == 0 and rows // r <= steps:
            return r
    raise ValueError((rows, steps))


def _pre_call(h, g1, win, wo, gm, wm, wa, wgt, layer, later_weights):
    t, d = h.shape
    tm = TOKEN_TILE
    ng = wgt.shape[0]
    steps = t // tm
    row = lambda n: pl.BlockSpec((tm, n), lambda i: (i, 0))
    weights = (g1, win, wo, gm, wm, wa, wgt)
    slabs_in, slabs_out = [], []
    for w in later_weights:
        r = _slab_rows(w.shape[1], steps)
        last = w.shape[1] // r - 1
        slabs_in.append(pl.BlockSpec((1, r, w.shape[2]), lambda i, last=last: (layer, jnp.minimum(i, last), 0)))
        slabs_out.append(pl.BlockSpec((r, w.shape[2]), lambda i, last=last: (jnp.minimum(i, last), 0)))
    outs = pl.pallas_call(
        _pre_kernel,
        grid=(steps,),
        in_specs=[row(d)] + [_resident(w.shape) for w in weights] + slabs_in,
        out_specs=[row(d), row(wm.shape[1]), row(wa.shape[1]), pl.BlockSpec((ng, tm), lambda i: (0, i))]
                  + slabs_out,
        out_shape=[jax.ShapeDtypeStruct((t, d), F32),
                   jax.ShapeDtypeStruct((t, wm.shape[1]), BF16),
                   jax.ShapeDtypeStruct((t, wa.shape[1]), BF16),
                   jax.ShapeDtypeStruct((ng, t), F32)]
                  + [jax.ShapeDtypeStruct(w.shape[1:], BF16) for w in later_weights],
        compiler_params=pltpu.CompilerParams(
            dimension_semantics=("arbitrary",), vmem_limit_bytes=VMEM_LIMIT_BYTES_V7X),
        name="pre_ffn_inproj",
    )(h, *weights, *later_weights)
    return outs[:4], outs[4:]


def _post_kernel(h_ref, hf_ref, hb_ref, og_ref, ya_ref, p_ref, gn_ref, wout_ref, g2_ref, win_ref, wo_ref,
                 gp_ref, wpg_ref, wpp_ref, gf_ref, o_ref, *, final_norm):
    dm = hf_ref.shape[1]
    h = h_ref[...] + _dot(ya_ref[...], wout_ref[dm:, :])
    heads = []
    for c0 in range(0, dm, M_DV):
        hh = hf_ref[:, c0:c0 + M_DV] + hb_ref[:, c0:c0 + M_DV]
        hn = hh * lax.rsqrt(jnp.mean(hh * hh, axis=-1, keepdims=True) + EPS) * gn_ref[:, c0:c0 + M_DV]
        heads.append((jax.nn.sigmoid(og_ref[:, c0:c0 + M_DV].astype(F32)) * hn).astype(BF16))
    h = h + _dot(jnp.concatenate(heads, axis=1), wout_ref[0:dm, :])
    xn = _rmsnorm(h, g2_ref[...]).astype(BF16)
    h = h + 0.5 * _swiglu(xn, win_ref, wo_ref)
    gate = jax.nn.sigmoid(_dot(_rmsnorm(h, gp_ref[...]).astype(BF16), wpg_ref[...]))
    h = h + gate * _dot(p_ref[...].astype(BF16), wpp_ref[...])
    if final_norm:
        h = _rmsnorm(h, gf_ref[...])
    o_ref[...] = h


def _post_call(h1, hf, hb, zm, og_block, ya, p, gn, wout, g2, win, wo, gp, wpg, wpp, gf, final_norm):
    t, d = h1.shape
    tm = TOKEN_TILE
    row = lambda n: pl.BlockSpec((tm, n), lambda i: (i, 0))
    weights = (gn, wout, g2, win, wo, gp, wpg, wpp, gf)
    return pl.pallas_call(
        functools.partial(_post_kernel, final_norm=final_norm),
        grid=(t // tm,),
        in_specs=[row(d), row(hf.shape[1]), row(hb.shape[1]),
                  pl.BlockSpec((tm, hf.shape[1]), lambda i: (i, og_block)), row(ya.shape[1]), row(p.shape[1])]
                 + [_resident(w.shape) for w in weights],
        out_specs=row(d),
        out_shape=jax.ShapeDtypeStruct((t, d), F32),
        compiler_params=pltpu.CompilerParams(
            dimension_semantics=("arbitrary",), vmem_limit_bytes=VMEM_LIMIT_BYTES_V7X),
        name="post_outproj_ffn_ple",
    )(h1, hf, hb, zm, ya, p, *weights)


def _attn_kernel(lam_ref, slope_ref, g_ref, q_ref, k_ref, v_ref, qn_ref, kn_ref, o_ref,
                 tab_s, vt_s, qq_s, kk_s, st0_s, st1_s, m0_s, *, lam_init, tq, kblk):
    s = k_ref.shape[1]
    nq = s // tq
    nkb = s // kblk
    first_row = pl.program_id(1) == 0

    @pl.when(first_row)
    def _():
        r = lax.broadcasted_iota(jnp.int32, tab_s.shape, 1)
        c = lax.broadcasted_iota(jnp.int32, tab_s.shape, 0)
        tab_s[...] = jnp.abs(r - c + (s - tq)).astype(F32) * slope_ref[0, :, 0:1]

    lam_v = lam_ref[...]
    lam = (jnp.exp(jnp.sum(lam_v[0:1] * lam_v[1:2], axis=1, keepdims=True))
           - jnp.exp(jnp.sum(lam_v[2:3] * lam_v[3:4], axis=1, keepdims=True)) + lam_init)

    vt_s[0:A_DV, :] = v_ref[0].astype(F32).T.astype(BF16)
    vt_s[A_DV:, :] = jnp.ones((vt_s.shape[0] - A_DV, s), BF16)
    qq_s[0:s, :] = q_ref[0]
    qq_s[s:, :] = qn_ref[0]
    kk_s[0] = k_ref[0]
    kk_s[1] = kn_ref[0]
    gain = g_ref[0] * (1.0 - lam_init)
    lane = lax.broadcasted_iota(jnp.int32, (tq, 2 * A_DK), 1)
    neg = jnp.full((1, 2 * tq), -jnp.inf, F32)
    acc0 = jnp.zeros((vt_s.shape[0], 2 * tq), F32)

    def stage(u, dst, m_prev, src):
        if isinstance(u, int):
            row, tile = u // nq, u % nq
        else:
            row, tile = jnp.where(u == nq, 1, 0), u & (nq - 1)
        q = qq_s[pl.ds(_aligned(u * tq, tq), tq), :]
        zero = jnp.zeros_like(q)
        qm = jnp.concatenate([jnp.where(lane < A_DK, q, zero), jnp.where(lane >= A_DK, q, zero)], axis=0)
        mx, acc = neg, acc0
        for kb in range(nkb):
            rows = slice(kb * kblk, (kb + 1) * kblk)
            bias = tab_s[pl.ds(_aligned((nq - 1 - tile) * tq + kb * kblk, kblk), kblk), :]
            sc = _dot_nt(kk_s[row, rows, :], qm) - jnp.concatenate([bias, bias], axis=1)
            dst[rows, :] = sc
            mx = jnp.maximum(mx, jnp.max(sc, axis=0, keepdims=True))
            if src is not None:
                e = jnp.exp2(src[rows, :] - m_prev)
                acc = acc + _dot(vt_s[:, rows], e.astype(BF16))
        return mx, acc

    def finish(u, acc):
        r = 1.0 / acc[A_DV:A_DV + 1, :]
        o = acc[:A_DV, :tq] * r[:, :tq] - lam * (acc[:A_DV, tq:] * r[:, tq:])
        on = o * lax.rsqrt(jnp.mean(o * o, axis=0, keepdims=True) + EPS)
        o_ref[0, pl.ds(_aligned(u * tq, tq), tq), :] = (on.T * gain).astype(o_ref.dtype)

    assert nq % 2 == 0 and nq & (nq - 1) == 0

    @pl.when(first_row)
    def _():
        m0_s[...] = stage(0, st0_s, None, None)[0]

    m1, acc_0 = stage(1, st1_s, m0_s[...], st0_s)
    finish(0, acc_0)
    carry = stage(2, st0_s, m1, st1_s)

    def body(j, carry):
        m_even, acc_odd = carry
        finish(2 * j - 1, acc_odd)
        m_odd, acc_even = stage(2 * j + 1, st1_s, m_even, st0_s)
        finish(2 * j, acc_even)
        return stage(2 * j + 2, st0_s, m_odd, st1_s)

    m_next, acc_last = lax.fori_loop(1, nq // 2, body, carry)
    finish(nq - 1, acc_last)
    m0_s[...] = m_next


def _attn_call(za, lam_vec, slopes, g_anorm, lam_init):
    b, s, _ = za.shape
    tq = ATTN_Q_TILE
    blk = 2 * A_DK
    assert blk == A_DV == LANES_V7X
    head_block = lambda off: pl.BlockSpec((1, s, blk), lambda h, i: (i, 0, off + h))
    nxt = lambda i: jnp.minimum(i + 1, b - 1)
    return pl.pallas_call(
        functools.partial(_attn_kernel, lam_init=lam_init, tq=tq, kblk=ATTN_KEY_BLOCK),
        grid=(A_HEADS, b),
        in_specs=[pl.BlockSpec(lam_vec.shape, lambda h, i: (0, 0)),
                  pl.BlockSpec((1, 1, LANES_V7X), lambda h, i: (h, 0, 0)),
                  pl.BlockSpec((1, 1, A_DV), lambda h, i: (h, 0, 0)),
                  head_block(0), head_block(A_HEADS), head_block(2 * A_HEADS),
                  pl.BlockSpec((1, tq, blk), lambda h, i: (nxt(i), 0, h)),
                  pl.BlockSpec((1, s, blk), lambda h, i: (nxt(i), 0, A_HEADS + h))],
        out_specs=pl.BlockSpec((1, s, A_DV), lambda h, i: (i, 0, h)),
        out_shape=jax.ShapeDtypeStruct((b, s, A_HEADS * A_DV), BF16),
        scratch_shapes=[pltpu.VMEM((2 * s - tq, tq), F32),
                        pltpu.VMEM((A_DV + BF16_ROWS_PER_VREG_V7X, s), BF16),
                        pltpu.VMEM((s + tq, blk), BF16),
                        pltpu.VMEM((2, s, blk), BF16),
                        pltpu.VMEM((s, 2 * tq), F32),
                        pltpu.VMEM((s, 2 * tq), F32),
                        pltpu.VMEM((1, 2 * tq), F32)],
        compiler_params=pltpu.CompilerParams(
            dimension_semantics=("arbitrary", "arbitrary"), vmem_limit_bytes=VMEM_LIMIT_BYTES_V7X),
        name="diff_attention",
    )(lam_vec, slopes, g_anorm, za, za, za, za, za)


def _log_sigmoid(x):
    return jnp.minimum(x, 0.0) - jnp.log(1.0 + jnp.exp(-jnp.abs(x)))


def _chunk_scan(x, op, fill, pos_in_chunk, chunk, reverse):
    n = x.shape[1]
    k = 1
    while k < chunk:
        if reverse:
            shifted = pltpu.roll(x, n - k, 1)
            valid = pos_in_chunk < chunk - k
        else:
            shifted = pltpu.roll(x, k, 1)
            valid = pos_in_chunk >= k
        x = op(x, jnp.where(valid, shifted, fill))
        k *= 2
    return x


def _lane_replicated(row):
    return jnp.broadcast_to(row, (LANES_V7X, row.shape[1])).T


def _mlstm_kernel(zm_ref, zgt_ref, convw_ref, bg_ref, hf_ref, hb_ref,
                  qx_s, kt_s, ktb_s, brow_s, cmrow_s, crow_s, c_s, m_s, *, chunk, rblk):
    s = zm_ref.shape[1]
    nc = s // chunk
    nseq = 2 * M_HEADS
    dqk = M_HEADS * M_DK

    cw = convw_ref[...]
    lane_q = lax.broadcasted_iota(jnp.int32, (rblk, 2 * M_DK), 1)
    for j in range(s // rblk):
        r0 = j * rblk
        halo = SUBLANES_V7X
        parts = []
        if j == 0:
            parts.append(jnp.zeros((halo, 2 * dqk), F32))
        else:
            parts.append(zm_ref[0, r0 - halo:r0, 0:2 * dqk].astype(F32))
        parts.append(zm_ref[0, r0:r0 + rblk, 0:2 * dqk].astype(F32))
        if j == s // rblk - 1:
            parts.append(jnp.zeros((halo, 2 * dqk), F32))
        else:
            parts.append(zm_ref[0, r0 + rblk:r0 + rblk + halo, 0:2 * dqk].astype(F32))
        xe = jnp.concatenate(parts, axis=0)
        ne = rblk + 2 * halo
        y = (pltpu.roll(xe, 1, 0) * cw[0:1] + xe * cw[1:2] + pltpu.roll(xe, ne - 1, 0) * cw[2:3])
        y = y[halo:halo + rblk]
        y = y * jax.nn.sigmoid(y)
        q = y[:, :dqk].astype(BF16)
        kk = y[:, dqk:] * (M_DK ** -0.5)
        kt = kk.T
        kt_s[:, r0:r0 + rblk] = kt
        ktb_s[:, r0:r0 + rblk] = kt.astype(BF16)
        for h in range(M_HEADS):
            pair = q[:, (h // 2) * 2 * M_DK:(h // 2 + 1) * 2 * M_DK]
            keep = (lane_q >= M_DK) if (h % 2) else (lane_q < M_DK)
            qx_s[r0:r0 + rblk, h * 2 * M_DK:(h + 1) * 2 * M_DK] = jnp.where(keep, pair, jnp.zeros_like(pair))

    g = zgt_ref[...] + bg_ref[...]
    li = g[0:nseq] * LOG2E
    lf = _log_sigmoid(g[nseq:2 * nseq]) * LOG2E
    pos = lax.broadcasted_iota(jnp.int32, li.shape, 1) % chunk
    is_fw = lax.broadcasted_iota(jnp.int32, li.shape, 0) < M_HEADS
    b = jnp.where(is_fw, _chunk_scan(lf, jnp.add, 0.0, pos, chunk, False),
                  _chunk_scan(lf, jnp.add, 0.0, pos, chunk, True))
    c = li - b
    cmax = jnp.where(is_fw, _chunk_scan(c, jnp.maximum, -jnp.inf, pos, chunk, False),
                     _chunk_scan(c, jnp.maximum, -jnp.inf, pos, chunk, True))
    for j in range(nc):
        cols = slice(j * chunk, (j + 1) * chunk)
        brow_s[j] = b[:, cols]
        cmrow_s[j] = cmax[:, cols]
        crow_s[j] = c[:, cols]

    c_s[...] = jnp.zeros_like(c_s)
    m_s[...] = jnp.zeros_like(m_s)

    rr = lax.broadcasted_iota(jnp.int32, (chunk, chunk), 0)
    cc = lax.broadcasted_iota(jnp.int32, (chunk, chunk), 1)
    causal = (cc <= rr, cc >= rr)
    ones = jnp.ones((chunk, M_DV), BF16)

    def step(j, carry):
        for d in range(2):
            cj = j if d == 0 else nc - 1 - j
            rows = pl.ds(pl.multiple_of(cj * chunk, chunk), chunk)
            last = chunk - 1 if d == 0 else 0
            h_out = hf_ref if d == 0 else hb_ref
            for pr in range(M_HEADS // 2):
                state = c_s[d, pr]
                state_bf = state.astype(BF16)
                ktp = ktb_s[pr * 2 * M_DK:(pr + 1) * 2 * M_DK, rows]
                for half in range(2):
                    h = 2 * pr + half
                    i = d * M_HEADS + h
                    blk = slice(h * M_DV, (h + 1) * M_DV)
                    qc = qx_s[rows, blk]
                    v_aug = jnp.concatenate(
                        [zm_ref[0, rows, 2 * dqk + h * M_DV:2 * dqk + (h + 1) * M_DV], ones], axis=1)
                    crow = crow_s[cj, i:i + 1, :]
                    m_old = m_s[i:i + 1, :]
                    b_rep = _lane_replicated(brow_s[cj, i:i + 1, :])
                    m_rep = jnp.maximum(m_old, _lane_replicated(cmrow_s[cj, i:i + 1, :]))
                    dt = jnp.where(causal[d], jnp.exp2(crow - m_rep), 0.0)
                    sm = (_dot(qc, ktp) * dt).astype(BF16)
                    qw = (qc * jnp.exp2(m_old - m_rep)).astype(BF16)
                    tot = _dot(jnp.concatenate([qw, sm], axis=1),
                               jnp.concatenate([state_bf, v_aug], axis=0))
                    floor = jnp.exp2(-(b_rep + m_rep))
                    h_out[0, rows, blk] = tot[:, :M_DV] / jnp.maximum(jnp.abs(tot[:, M_DV:]), floor)
                    m_last = m_rep[last:last + 1, :]
                    wk = jnp.exp2(crow - m_last[:, 0:1])
                    kth = kt_s[h * M_DK:(h + 1) * M_DK, rows]
                    dc = _dot((kth * wk).astype(BF16), v_aug)
                    sub = slice(half * M_DK, (half + 1) * M_DK)
                    c_s[d, pr, sub, :] = jnp.exp2(m_old - m_last)[:, 0:1] * state[sub] + dc
                    m_s[i:i + 1, :] = b_rep[last:last + 1, :] + m_last
        return carry

    def steps(jj, carry):
        for k in range(MLSTM_CHUNKS_PER_STEP):
            step(jj * MLSTM_CHUNKS_PER_STEP + k, carry)
        return carry

    lax.fori_loop(0, nc // MLSTM_CHUNKS_PER_STEP, steps, 0)


def _mlstm_call(zm, zgt, conv_w, bg):
    b, s, wz = zm.shape
    chunk = MLSTM_CHUNK
    nc = s // chunk
    dv = M_HEADS * M_DV
    dqk = M_HEADS * M_DK
    nseq = 2 * M_HEADS
    full = lambda a: pl.BlockSpec(a.shape, lambda i: (0,) * a.ndim)
    per_b = lambda n: pl.BlockSpec((1, s, n), lambda i: (i, 0, 0))
    gate_rows = lambda: pltpu.VMEM((nc, nseq, chunk), F32)
    return pl.pallas_call(
        functools.partial(_mlstm_kernel, chunk=chunk, rblk=MLSTM_ROW_BLOCK),
        grid=(b,),
        in_specs=[per_b(wz), pl.BlockSpec((zgt.shape[0], s), lambda i: (0, i)), full(conv_w), full(bg)],
        out_specs=[per_b(dv), per_b(dv)],
        out_shape=[jax.ShapeDtypeStruct((b, s, dv), F32), jax.ShapeDtypeStruct((b, s, dv), F32)],
        scratch_shapes=[
            pltpu.VMEM((s, M_HEADS * 2 * M_DK), BF16),
            pltpu.VMEM((dqk, s), F32),
            pltpu.VMEM((dqk, s), BF16),
            gate_rows(),
            gate_rows(),
            gate_rows(),
            pltpu.VMEM((2, M_HEADS // 2, 2 * M_DK, 2 * M_DV), F32),
            pltpu.VMEM((nseq, LANES_V7X), F32),
        ],
        compiler_params=pltpu.CompilerParams(
            dimension_semantics=("arbitrary",), vmem_limit_bytes=VMEM_LIMIT_BYTES_V7X),
        name="mlstm_bidir",
    )(zm, zgt, conv_w, bg)


def _gate_rows(a16):
    hh = M_HEADS
    return jnp.concatenate([a16[..., 0:hh], a16[..., 2 * hh:3 * hh], a16[..., hh:2 * hh], a16[..., 3 * hh:]],
                           axis=-1)


def kernel(x, p, g_ffn1, w_ffn1_in, w_ffn1_out, g_mix, w_in, b_mgate, conv_w, g_mnorm, lam_q1, lam_k1, lam_q2, lam_k2, g_anorm, w_out, g_ffn2, w_ffn2_in, w_ffn2_out, g_ple, w_ple_gate, w_ple_proj, g_final):
    bsz, s, d = x.shape
    depth = w_in.shape[0]
    t = bsz * s
    m_qk, m_v, m_g = M_HEADS * M_DK, M_HEADS * M_DV, 4 * M_HEADS
    a_qk = A_HEADS * 2 * A_DK
    m_end = 2 * m_qk + 2 * m_v
    slopes = np.array([2.0 ** (-8.0 * (h + 1) / A_HEADS) for h in range(A_HEADS)], dtype=np.float32)
    slopes = jnp.broadcast_to(jnp.asarray(slopes * np.float32(LOG2E))[:, None, None],
                              (A_HEADS, 1, LANES_V7X))
    row = lambda v: v.reshape(1, -1).astype(F32)

    h = x.reshape(t, d)
    for i in range(depth):
        lam_init = 0.8 - 0.6 * math.exp(-0.3 * i)
        wi = w_in[i]
        wm = wi[:, :m_end].astype(BF16)
        wa = wi[:, m_end + m_g:]
        wa = jnp.concatenate([wa[:, :a_qk] * (A_DK ** -0.5 * LOG2E), wa[:, a_qk:]], axis=1).astype(BF16)
        wgt = _gate_rows(wi[:, m_end:m_end + m_g]).T.astype(BF16)
        bg = _gate_rows(b_mgate[i].astype(F32)).reshape(m_g, 1)

        (h1, zm, za, zgt), (win2, wo2, wout, wpg, wpp) = _pre_call(
            h, row(g_ffn1[i]), w_ffn1_in[i].astype(BF16), w_ffn1_out[i].astype(BF16), row(g_mix[i]), wm, wa, wgt,
            i, (w_ffn2_in, w_ffn2_out, w_out, w_ple_gate, w_ple_proj))
        hf, hb = _mlstm_call(zm.reshape(bsz, s, -1), zgt, conv_w[i].astype(F32), bg)
        lam_vec = jnp.stack([lam_q1[i], lam_k1[i], lam_q2[i], lam_k2[i]]).astype(F32)
        ya = _attn_call(za.reshape(bsz, s, -1), lam_vec, slopes,
                        g_anorm[i].reshape(A_HEADS, 1, A_DV).astype(F32), lam_init)
        assert (2 * m_qk + m_v) % m_v == 0
        h = _post_call(h1, hf.reshape(t, -1), hb.reshape(t, -1), zm, (2 * m_qk + m_v) // m_v, ya.reshape(t, -1),
                       p[i].reshape(t, -1), row(g_mnorm[i]),
                       wout, row(g_ffn2[i]), win2, wo2, row(g_ple[i]), wpg, wpp, row(g_final),
                       final_norm=(i == depth - 1))
    return h.reshape(bsz, s, d)
```

```python
import functools
import math

import jax
import jax.numpy as jnp
import numpy as np
from jax import lax
from jax.experimental import pallas as pl
from jax.experimental.pallas import tpu as pltpu

F32 = jnp.float32
BF16 = jnp.bfloat16

EPS = 1e-6
LOG2E = 1.4426950408889634

M_HEADS = 4
M_DK = 64
M_DV = 128
A_HEADS = 4
A_DK = 64
A_DV = 128
CONV_K = 3

LANES_V7X = 128
SUBLANES_V7X = 8
BF16_ROWS_PER_VREG_V7X = 16
VMEM_LIMIT_BYTES_V7X = 56 * 1024 * 1024

TOKEN_TILE = 512
FF_CHUNK = 256
MLSTM_CHUNK = 128
MLSTM_CHUNKS_PER_STEP = 8
MLSTM_ROW_BLOCK = 128
ATTN_Q_TILE = 256
ATTN_KEY_BLOCK = 256


def _aligned(x, m):
    return x if isinstance(x, int) else pl.multiple_of(x, m)


def _dot(a, b):
    return jnp.dot(a, b, preferred_element_type=F32)


def _dot_nt(a, b):
    return lax.dot_general(a, b, (((1,), (1,)), ((), ())), preferred_element_type=F32)


def _rmsnorm(x, g):
    ms = jnp.mean(x * x, axis=-1, keepdims=True)
    return x * lax.rsqrt(ms + EPS) * g


def _swiglu(xn, win_ref, wo_ref):
    f = wo_ref.shape[0]
    parts = []
    for c0 in range(0, f, FF_CHUNK):
        g = _dot(xn, win_ref[:, c0:c0 + FF_CHUNK])
        u = _dot(xn, win_ref[:, f + c0:f + c0 + FF_CHUNK])
        parts.append((g * jax.nn.sigmoid(g) * u).astype(BF16))
    return _dot(jnp.concatenate(parts, axis=1), wo_ref[...])


def _resident(shape):
    nd = len(shape)
    return pl.BlockSpec(shape, lambda *_: (0,) * nd, pipeline_mode=pl.Buffered(1))


def _pre_kernel(x_ref, g1_ref, win_ref, wo_ref, gm_ref, wm_ref, wa_ref, wgt_ref, *rest):
    n_cast = (len(rest) - 5) // 2
    cast_in, (h_ref, zm_ref, za_ref, vt_ref, zgt_ref), cast_out = (
        rest[:n_cast], rest[n_cast:n_cast + 5], rest[n_cast + 5:])
    x = x_ref[...]
    xn = _rmsnorm(x, g1_ref[...]).astype(BF16)
    h = x + 0.5 * _swiglu(xn, win_ref, wo_ref)
    h_ref[...] = h
    u = _rmsnorm(h, gm_ref[...]).astype(BF16)
    zm_ref[...] = _dot(u, wm_ref[...]).astype(BF16)
    za = _dot(u, wa_ref[...])
    nqk = za_ref.shape[1]
    za_ref[...] = za[:, :nqk].astype(BF16)
    vt = za[:, nqk:].T.astype(BF16)
    one = jnp.ones((BF16_ROWS_PER_VREG_V7X, vt.shape[1]), BF16)
    vt_ref[...] = jnp.concatenate(
        [blk for c0 in range(0, vt.shape[0], A_DV) for blk in (vt[c0:c0 + A_DV], one)], axis=0)
    zgt_ref[...] = _dot_nt(wgt_ref[...], u)
    for src, dst in zip(cast_in, cast_out):
        dst[...] = src[0].astype(BF16)


def _slab_rows(rows, steps):
    for r in range(BF16_ROWS_PER_VREG_V7X, rows + 1, BF16_ROWS_PER_VREG_V7X):
        if rows % r == 0 and rows // r <= steps:
            return r
    raise ValueError((rows, steps))


def _pre_call(h, g1, win, wo, gm, wm, wa, wgt, layer, later_weights):
    t, d = h.shape
    tm = TOKEN_TILE
    ng = wgt.shape[0]
    n_av = A_HEADS * A_DV
    n_vt = A_HEADS * (A_DV + BF16_ROWS_PER_VREG_V7X)
    steps = t // tm
    row = lambda n: pl.BlockSpec((tm, n), lambda i: (i, 0))
    weights = (g1, win, wo, gm, wm, wa, wgt)
    slabs_in, slabs_out = [], []
    for w in later_weights:
        r = _slab_rows(w.shape[1], steps)
        last = w.shape[1] // r - 1
        slabs_in.append(pl.BlockSpec((1, r, w.shape[2]), lambda i, last=last: (layer, jnp.minimum(i, last), 0)))
        slabs_out.append(pl.BlockSpec((r, w.shape[2]), lambda i, last=last: (jnp.minimum(i, last), 0)))
    outs = pl.pallas_call(
        _pre_kernel,
        grid=(steps,),
        in_specs=[row(d)] + [_resident(w.shape) for w in weights] + slabs_in,
        out_specs=[row(d), row(wm.shape[1]), row(wa.shape[1] - n_av),
                   pl.BlockSpec((n_vt, tm), lambda i: (0, i)), pl.BlockSpec((ng, tm), lambda i: (0, i))]
                  + slabs_out,
        out_shape=[jax.ShapeDtypeStruct((t, d), F32),
                   jax.ShapeDtypeStruct((t, wm.shape[1]), BF16),
                   jax.ShapeDtypeStruct((t, wa.shape[1] - n_av), BF16),
                   jax.ShapeDtypeStruct((n_vt, t), BF16),
                   jax.ShapeDtypeStruct((ng, t), F32)]
                  + [jax.ShapeDtypeStruct(w.shape[1:], BF16) for w in later_weights],
        compiler_params=pltpu.CompilerParams(
            dimension_semantics=("arbitrary",), vmem_limit_bytes=VMEM_LIMIT_BYTES_V7X),
        name="pre_ffn_inproj",
    )(h, *weights, *later_weights)
    return outs[:5], outs[5:]


def _post_kernel(h_ref, hf_ref, hb_ref, og_ref, ya_ref, p_ref, gn_ref, wout_ref, g2_ref, win_ref, wo_ref,
                 gp_ref, wpg_ref, wpp_ref, gf_ref, o_ref, *, final_norm):
    dm = hf_ref.shape[1]
    h = h_ref[...] + _dot(ya_ref[...], wout_ref[dm:, :])
    heads = []
    for c0 in range(0, dm, M_DV):
        hh = hf_ref[:, c0:c0 + M_DV] + hb_ref[:, c0:c0 + M_DV]
        hn = hh * lax.rsqrt(jnp.mean(hh * hh, axis=-1, keepdims=True) + EPS) * gn_ref[:, c0:c0 + M_DV]
        heads.append((jax.nn.sigmoid(og_ref[:, c0:c0 + M_DV].astype(F32)) * hn).astype(BF16))
    h = h + _dot(jnp.concatenate(heads, axis=1), wout_ref[0:dm, :])
    xn = _rmsnorm(h, g2_ref[...]).astype(BF16)
    h = h + 0.5 * _swiglu(xn, win_ref, wo_ref)
    gate = jax.nn.sigmoid(_dot(_rmsnorm(h, gp_ref[...]).astype(BF16), wpg_ref[...]))
    h = h + gate * _dot(p_ref[...].astype(BF16), wpp_ref[...])
    if final_norm:
        h = _rmsnorm(h, gf_ref[...])
    o_ref[...] = h


def _post_call(h1, hf, hb, zm, og_block, ya, p, gn, wout, g2, win, wo, gp, wpg, wpp, gf, final_norm):
    t, d = h1.shape
    tm = TOKEN_TILE
    row = lambda n: pl.BlockSpec((tm, n), lambda i: (i, 0))
    weights = (gn, wout, g2, win, wo, gp, wpg, wpp, gf)
    return pl.pallas_call(
        functools.partial(_post_kernel, final_norm=final_norm),
        grid=(t // tm,),
        in_specs=[row(d), row(hf.shape[1]), row(hb.shape[1]),
                  pl.BlockSpec((tm, hf.shape[1]), lambda i: (i, og_block)), row(ya.shape[1]), row(p.shape[1])]
                 + [_resident(w.shape) for w in weights],
        out_specs=row(d),
        out_shape=jax.ShapeDtypeStruct((t, d), F32),
        compiler_params=pltpu.CompilerParams(
            dimension_semantics=("arbitrary",), vmem_limit_bytes=VMEM_LIMIT_BYTES_V7X),
        name="post_outproj_ffn_ple",
    )(h1, hf, hb, zm, ya, p, *weights)


def _attn_kernel(lam_ref, slope_ref, g_ref, q_ref, k_ref, vt_ref, qn_ref, kn_ref, o_ref,
                 tab_s, st0_s, st1_s, m0_s, *, lam_init, tq, kblk):
    s = k_ref.shape[1]
    nq = s // tq
    nkb = s // kblk
    first_row = pl.program_id(1) == 0

    @pl.when(first_row)
    def _():
        r = lax.broadcasted_iota(jnp.int32, tab_s.shape, 1)
        c = lax.broadcasted_iota(jnp.int32, tab_s.shape, 0)
        tab_s[...] = jnp.abs(r - c + (s - tq)).astype(F32) * slope_ref[0, :, 0:1]

    lam_v = lam_ref[...]
    lam = (jnp.exp(jnp.sum(lam_v[0:1] * lam_v[1:2], axis=1, keepdims=True))
           - jnp.exp(jnp.sum(lam_v[2:3] * lam_v[3:4], axis=1, keepdims=True)) + lam_init)

    gain = g_ref[0] * (1.0 - lam_init)
    lane = lax.broadcasted_iota(jnp.int32, (tq, 2 * A_DK), 1)
    neg = jnp.full((1, 2 * tq), -jnp.inf, F32)
    acc0 = jnp.zeros((vt_ref.shape[0], 2 * tq), F32)

    def stage(tile, q, keys_ref, dst, m_prev, src):
        zero = jnp.zeros_like(q)
        qm = jnp.concatenate([jnp.where(lane < A_DK, q, zero), jnp.where(lane >= A_DK, q, zero)], axis=0)
        mx, acc = neg, acc0
        for kb in range(nkb):
            rows = slice(kb * kblk, (kb + 1) * kblk)
            bias = tab_s[pl.ds(_aligned((nq - 1 - tile) * tq + kb * kblk, kblk), kblk), :]
            sc = _dot_nt(keys_ref[0, rows, :], qm) - jnp.concatenate([bias, bias], axis=1)
            dst[rows, :] = sc
            mx = jnp.maximum(mx, jnp.max(sc, axis=0, keepdims=True))
            if src is not None:
                e = jnp.exp2(src[rows, :] - m_prev).astype(BF16)
                acc = acc + _dot(vt_ref[:, rows], e)
        return mx, acc

    def finish(u, acc):
        r = 1.0 / acc[A_DV:A_DV + 1, :]
        o = acc[:A_DV, :tq] * r[:, :tq] - lam * (acc[:A_DV, tq:] * r[:, tq:])
        on = o * lax.rsqrt(jnp.mean(o * o, axis=0, keepdims=True) + EPS)
        o_ref[0, pl.ds(_aligned(u * tq, tq), tq), :] = (on.T * gain).astype(o_ref.dtype)

    assert nq % 2 == 0 and nq >= 4

    def q_tile(u):
        return q_ref[0, pl.ds(_aligned(u * tq, tq), tq), :]

    @pl.when(first_row)
    def _():
        m0_s[...] = stage(0, q_tile(0), k_ref, st0_s, None, None)[0]

    def pair(j, carry, last):
        m_even, acc_odd = carry
        if acc_odd is not None:
            finish(2 * j - 1, acc_odd)
        m_odd, acc_even = stage(2 * j + 1, q_tile(2 * j + 1), k_ref, st1_s, m_even, st0_s)
        finish(2 * j, acc_even)
        if last:
            return stage(0, qn_ref[0], kn_ref, st0_s, m_odd, st1_s)
        return stage(2 * j + 2, q_tile(2 * j + 2), k_ref, st0_s, m_odd, st1_s)

    carry = pair(0, (m0_s[...], None), False)
    carry = lax.fori_loop(1, nq // 2 - 1, lambda j, c: pair(j, c, False), carry)
    m_next, acc_last = pair(nq // 2 - 1, carry, True)
    finish(nq - 1, acc_last)
    m0_s[...] = m_next


def _attn_call(za, vt, lam_vec, slopes, g_anorm, lam_init):
    b, s, _ = za.shape
    tq = ATTN_Q_TILE
    blk = 2 * A_DK
    assert blk == A_DV == LANES_V7X
    head_block = lambda off: pl.BlockSpec((1, s, blk), lambda h, i: (i, 0, off + h))
    nxt = lambda i: jnp.minimum(i + 1, b - 1)
    return pl.pallas_call(
        functools.partial(_attn_kernel, lam_init=lam_init, tq=tq, kblk=ATTN_KEY_BLOCK),
        grid=(A_HEADS, b),
        in_specs=[pl.BlockSpec(lam_vec.shape, lambda h, i: (0, 0)),
                  pl.BlockSpec((1, 1, LANES_V7X), lambda h, i: (h, 0, 0)),
                  pl.BlockSpec((1, 1, A_DV), lambda h, i: (h, 0, 0)),
                  head_block(0), head_block(A_HEADS),
                  pl.BlockSpec((A_DV + BF16_ROWS_PER_VREG_V7X, s), lambda h, i: (h, i)),
                  pl.BlockSpec((1, tq, blk), lambda h, i: (nxt(i), 0, h)),
                  pl.BlockSpec((1, s, blk), lambda h, i: (nxt(i), 0, A_HEADS + h))],
        out_specs=pl.BlockSpec((1, s, A_DV), lambda h, i: (i, 0, h)),
        out_shape=jax.ShapeDtypeStruct((b, s, A_HEADS * A_DV), BF16),
        scratch_shapes=[pltpu.VMEM((2 * s - tq, tq), F32),
                        pltpu.VMEM((s, 2 * tq), F32),
                        pltpu.VMEM((s, 2 * tq), F32),
                        pltpu.VMEM((1, 2 * tq), F32)],
        compiler_params=pltpu.CompilerParams(
            dimension_semantics=("arbitrary", "arbitrary"), vmem_limit_bytes=VMEM_LIMIT_BYTES_V7X),
        name="diff_attention",
    )(lam_vec, slopes, g_anorm, za, za, vt, za, za)


def _log_sigmoid(x):
    return jnp.minimum(x, 0.0) - jnp.log(1.0 + jnp.exp(-jnp.abs(x)))


def _chunk_scan(x, op, fill, pos_in_chunk, chunk, reverse):
    n = x.shape[1]
    k = 1
    while k < chunk:
        if reverse:
            shifted = pltpu.roll(x, n - k, 1)
            valid = pos_in_chunk < chunk - k
        else:
            shifted = pltpu.roll(x, k, 1)
            valid = pos_in_chunk >= k
        x = op(x, jnp.where(valid, shifted, fill))
        k *= 2
    return x


def _lane_replicated(row):
    return jnp.broadcast_to(row, (LANES_V7X, row.shape[1])).T


def _mlstm_kernel(zm_ref, zgt_ref, convw_ref, bg_ref, hf_ref, hb_ref,
                  qx_s, kt_s, ktb_s, brow_s, cmrow_s, crow_s, c_s, m_s, *, chunk, rblk):
    s = zm_ref.shape[1]
    nc = s // chunk
    nseq = 2 * M_HEADS
    dqk = M_HEADS * M_DK

    cw = convw_ref[...]
    lane_q = lax.broadcasted_iota(jnp.int32, (rblk, 2 * M_DK), 1)
    for j in range(s // rblk):
        r0 = j * rblk
        halo = SUBLANES_V7X
        parts = []
        if j == 0:
            parts.append(jnp.zeros((halo, 2 * dqk), F32))
        else:
            parts.append(zm_ref[0, r0 - halo:r0, 0:2 * dqk].astype(F32))
        parts.append(zm_ref[0, r0:r0 + rblk, 0:2 * dqk].astype(F32))
        if j == s // rblk - 1:
            parts.append(jnp.zeros((halo, 2 * dqk), F32))
        else:
            parts.append(zm_ref[0, r0 + rblk:r0 + rblk + halo, 0:2 * dqk].astype(F32))
        xe = jnp.concatenate(parts, axis=0)
        ne = rblk + 2 * halo
        y = (pltpu.roll(xe, 1, 0) * cw[0:1] + xe * cw[1:2] + pltpu.roll(xe, ne - 1, 0) * cw[2:3])
        y = y[halo:halo + rblk]
        y = y * jax.nn.sigmoid(y)
        q = y[:, :dqk].astype(BF16)
        kk = y[:, dqk:] * (M_DK ** -0.5)
        kt = kk.T
        kt_s[:, r0:r0 + rblk] = kt
        ktb_s[:, r0:r0 + rblk] = kt.astype(BF16)
        for h in range(M_HEADS):
            pair = q[:, (h // 2) * 2 * M_DK:(h // 2 + 1) * 2 * M_DK]
            keep = (lane_q >= M_DK) if (h % 2) else (lane_q < M_DK)
            qx_s[r0:r0 + rblk, h * 2 * M_DK:(h + 1) * 2 * M_DK] = jnp.where(keep, pair, jnp.zeros_like(pair))

    g = zgt_ref[...] + bg_ref[...]
    li = g[0:nseq] * LOG2E
    lf = _log_sigmoid(g[nseq:2 * nseq]) * LOG2E
    pos = lax.broadcasted_iota(jnp.int32, li.shape, 1) % chunk
    is_fw = lax.broadcasted_iota(jnp.int32, li.shape, 0) < M_HEADS
    b = jnp.where(is_fw, _chunk_scan(lf, jnp.add, 0.0, pos, chunk, False),
                  _chunk_scan(lf, jnp.add, 0.0, pos, chunk, True))
    c = li - b
    cmax = jnp.where(is_fw, _chunk_scan(c, jnp.maximum, -jnp.inf, pos, chunk, False),
                     _chunk_scan(c, jnp.maximum, -jnp.inf, pos, chunk, True))
    for j in range(nc):
        cols = slice(j * chunk, (j + 1) * chunk)
        brow_s[j] = b[:, cols]
        cmrow_s[j] = cmax[:, cols]
        crow_s[j] = c[:, cols]

    c_s[...] = jnp.zeros_like(c_s)
    m_s[...] = jnp.zeros_like(m_s)

    rr = lax.broadcasted_iota(jnp.int32, (chunk, chunk), 0)
    cc = lax.broadcasted_iota(jnp.int32, (chunk, chunk), 1)
    causal = (cc <= rr, cc >= rr)
    ones = jnp.ones((chunk, M_DV), BF16)

    def step(j, carry):
        for d in range(2):
            cj = j if d == 0 else nc - 1 - j
            rows = pl.ds(pl.multiple_of(cj * chunk, chunk), chunk)
            last = chunk - 1 if d == 0 else 0
            h_out = hf_ref if d == 0 else hb_ref
            for pr in range(M_HEADS // 2):
                state = c_s[d, pr]
                state_bf = state.astype(BF16)
                ktp = ktb_s[pr * 2 * M_DK:(pr + 1) * 2 * M_DK, rows]
                for half in range(2):
                    h = 2 * pr + half
                    i = d * M_HEADS + h
                    blk = slice(h * M_DV, (h + 1) * M_DV)
                    qc = qx_s[rows, blk]
                    v_aug = jnp.concatenate(
                        [zm_ref[0, rows, 2 * dqk + h * M_DV:2 * dqk + (h + 1) * M_DV], ones], axis=1)
                    crow = crow_s[cj, i:i + 1, :]
                    m_old = m_s[i:i + 1, :]
                    b_rep = _lane_replicated(brow_s[cj, i:i + 1, :])
                    m_rep = jnp.maximum(m_old, _lane_replicated(cmrow_s[cj, i:i + 1, :]))
                    dt = jnp.where(causal[d], jnp.exp2(crow - m_rep), 0.0)
                    sm = (_dot(qc, ktp) * dt).astype(BF16)
                    qw = (qc * jnp.exp2(m_old - m_rep)).astype(BF16)
                    tot = _dot(jnp.concatenate([qw, sm], axis=1),
                               jnp.concatenate([state_bf, v_aug], axis=0))
                    floor = jnp.exp2(-(b_rep + m_rep))
                    h_out[0, rows, blk] = tot[:, :M_DV] / jnp.maximum(jnp.abs(tot[:, M_DV:]), floor)
                    m_last = m_rep[last:last + 1, :]
                    wk = jnp.exp2(crow - m_last[:, 0:1])
                    kth = kt_s[h * M_DK:(h + 1) * M_DK, rows]
                    dc = _dot((kth * wk).astype(BF16), v_aug)
                    sub = slice(half * M_DK, (half + 1) * M_DK)
                    c_s[d, pr, sub, :] = jnp.exp2(m_old - m_last)[:, 0:1] * state[sub] + dc
                    m_s[i:i + 1, :] = b_rep[last:last + 1, :] + m_last
        return carry

    def steps(jj, carry):
        for k in range(MLSTM_CHUNKS_PER_STEP):
            step(jj * MLSTM_CHUNKS_PER_STEP + k, carry)
        return carry

    lax.fori_loop(0, nc // MLSTM_CHUNKS_PER_STEP, steps, 0)


def _mlstm_call(zm, zgt, conv_w, bg):
    b, s, wz = zm.shape
    chunk = MLSTM_CHUNK
    nc = s // chunk
    dv = M_HEADS * M_DV
    dqk = M_HEADS * M_DK
    nseq = 2 * M_HEADS
    full = lambda a: pl.BlockSpec(a.shape, lambda i: (0,) * a.ndim)
    per_b = lambda n: pl.BlockSpec((1, s, n), lambda i: (i, 0, 0))
    gate_rows = lambda: pltpu.VMEM((nc, nseq, chunk), F32)
    return pl.pallas_call(
        functools.partial(_mlstm_kernel, chunk=chunk, rblk=MLSTM_ROW_BLOCK),
        grid=(b,),
        in_specs=[per_b(wz), pl.BlockSpec((zgt.shape[0], s), lambda i: (0, i)), full(conv_w), full(bg)],
        out_specs=[per_b(dv), per_b(dv)],
        out_shape=[jax.ShapeDtypeStruct((b, s, dv), F32), jax.ShapeDtypeStruct((b, s, dv), F32)],
        scratch_shapes=[
            pltpu.VMEM((s, M_HEADS * 2 * M_DK), BF16),
            pltpu.VMEM((dqk, s), F32),
            pltpu.VMEM((dqk, s), BF16),
            gate_rows(),
            gate_rows(),
            gate_rows(),
            pltpu.VMEM((2, M_HEADS // 2, 2 * M_DK, 2 * M_DV), F32),
            pltpu.VMEM((nseq, LANES_V7X), F32),
        ],
        compiler_params=pltpu.CompilerParams(
            dimension_semantics=("arbitrary",), vmem_limit_bytes=VMEM_LIMIT_BYTES_V7X),
        name="mlstm_bidir",
    )(zm, zgt, conv_w, bg)


def _gate_rows(a16):
    hh = M_HEADS
    return jnp.concatenate([a16[..., 0:hh], a16[..., 2 * hh:3 * hh], a16[..., hh:2 * hh], a16[..., 3 * hh:]],
                           axis=-1)


def kernel(x, p, g_ffn1, w_ffn1_in, w_ffn1_out, g_mix, w_in, b_mgate, conv_w, g_mnorm, lam_q1, lam_k1, lam_q2, lam_k2, g_anorm, w_out, g_ffn2, w_ffn2_in, w_ffn2_out, g_ple, w_ple_gate, w_ple_proj, g_final):
    bsz, s, d = x.shape
    depth = w_in.shape[0]
    t = bsz * s
    m_qk, m_v, m_g = M_HEADS * M_DK, M_HEADS * M_DV, 4 * M_HEADS
    a_qk = A_HEADS * 2 * A_DK
    m_end = 2 * m_qk + 2 * m_v
    slopes = np.array([2.0 ** (-8.0 * (h + 1) / A_HEADS) for h in range(A_HEADS)], dtype=np.float32)
    slopes = jnp.broadcast_to(jnp.asarray(slopes * np.float32(LOG2E))[:, None, None],
                              (A_HEADS, 1, LANES_V7X))
    row = lambda v: v.reshape(1, -1).astype(F32)

    h = x.reshape(t, d)
    for i in range(depth):
        lam_init = 0.8 - 0.6 * math.exp(-0.3 * i)
        wi = w_in[i]
        wm = wi[:, :m_end].astype(BF16)
        wa = wi[:, m_end + m_g:]
        wa = jnp.concatenate([wa[:, :a_qk] * (A_DK ** -0.5 * LOG2E), wa[:, a_qk:]], axis=1).astype(BF16)
        wgt = _gate_rows(wi[:, m_end:m_end + m_g]).T.astype(BF16)
        bg = _gate_rows(b_mgate[i].astype(F32)).reshape(m_g, 1)

        (h1, zm, za, vt, zgt), (win2, wo2, wout, wpg, wpp) = _pre_call(
            h, row(g_ffn1[i]), w_ffn1_in[i].astype(BF16), w_ffn1_out[i].astype(BF16), row(g_mix[i]), wm, wa, wgt,
            i, (w_ffn2_in, w_ffn2_out, w_out, w_ple_gate, w_ple_proj))
        hf, hb = _mlstm_call(zm.reshape(bsz, s, -1), zgt, conv_w[i].astype(F32), bg)
        lam_vec = jnp.stack([lam_q1[i], lam_k1[i], lam_q2[i], lam_k2[i]]).astype(F32)
        ya = _attn_call(za.reshape(bsz, s, -1), vt, lam_vec, slopes,
                        g_anorm[i].reshape(A_HEADS, 1, A_DV).astype(F32), lam_init)
        assert (2 * m_qk + m_v) % m_v == 0
        h = _post_call(h1, hf.reshape(t, -1), hb.reshape(t, -1), zm, (2 * m_qk + m_v) // m_v, ya.reshape(t, -1),
                       p[i].reshape(t, -1), row(g_mnorm[i]),
                       wout, row(g_ffn2[i]), win2, wo2, row(g_ple[i]), wpg, wpp, row(g_final),
                       final_norm=(i == depth - 1))
    return h.reshape(bsz, s, d)
```

```python
import functools
import math

import jax
import jax.numpy as jnp
import numpy as np
from jax import lax
from jax.experimental import pallas as pl
from jax.experimental.pallas import tpu as pltpu

F32 = jnp.float32
BF16 = jnp.bfloat16

EPS = 1e-6
LOG2E = 1.4426950408889634

M_HEADS = 4
M_DK = 64
M_DV = 128
A_HEADS = 4
A_DK = 64
A_DV = 128
CONV_K = 3

LANES_V7X = 128
SUBLANES_V7X = 8
BF16_ROWS_PER_VREG_V7X = 16
VMEM_LIMIT_BYTES_V7X = 56 * 1024 * 1024

TOKEN_TILE = 512
FF_CHUNK = 256
MLSTM_CHUNK = 128
MLSTM_CHUNKS_PER_STEP = 8
MLSTM_ROW_BLOCK = 128
ATTN_Q_TILE = 256
ATTN_KEY_BLOCK = 256


def _aligned(x, m):
    return x if isinstance(x, int) else pl.multiple_of(x, m)


def _dot(a, b):
    return jnp.dot(a, b, preferred_element_type=F32)


def _dot_nt(a, b):
    return lax.dot_general(a, b, (((1,), (1,)), ((), ())), preferred_element_type=F32)


def _rmsnorm(x, g):
    ms = jnp.mean(x * x, axis=-1, keepdims=True)
    return x * lax.rsqrt(ms + EPS) * g


def _swiglu(xn, win_ref, wo_ref):
    f = wo_ref.shape[0]
    parts = []
    for c0 in range(0, f, FF_CHUNK):
        g = _dot(xn, win_ref[:, c0:c0 + FF_CHUNK])
        u = _dot(xn, win_ref[:, f + c0:f + c0 + FF_CHUNK])
        parts.append((g * jax.nn.sigmoid(g) * u).astype(BF16))
    return _dot(jnp.concatenate(parts, axis=1), wo_ref[...])


def _resident(shape):
    nd = len(shape)
    return pl.BlockSpec(shape, lambda *_: (0,) * nd, pipeline_mode=pl.Buffered(1))


def _pre_kernel(x_ref, g1_ref, win_ref, wo_ref, gm_ref, wm_ref, wa_ref, wgt_ref, *rest):
    n_cast = (len(rest) - 5) // 2
    cast_in, (h_ref, zm_ref, za_ref, vt_ref, zgt_ref), cast_out = (
        rest[:n_cast], rest[n_cast:n_cast + 5], rest[n_cast + 5:])
    x = x_ref[...]
    xn = _rmsnorm(x, g1_ref[...]).astype(BF16)
    h = x + 0.5 * _swiglu(xn, win_ref, wo_ref)
    h_ref[...] = h
    u = _rmsnorm(h, gm_ref[...]).astype(BF16)
    zm_ref[...] = _dot(u, wm_ref[...]).astype(BF16)
    za = _dot(u, wa_ref[...])
    nqk = za_ref.shape[1]
    za_ref[...] = za[:, :nqk].astype(BF16)
    vt = za[:, nqk:].T.astype(BF16)
    one = jnp.ones((BF16_ROWS_PER_VREG_V7X, vt.shape[1]), BF16)
    vt_ref[...] = jnp.concatenate(
        [blk for c0 in range(0, vt.shape[0], A_DV) for blk in (vt[c0:c0 + A_DV], one)], axis=0)
    zgt_ref[...] = _dot_nt(wgt_ref[...], u)
    for src, dst in zip(cast_in, cast_out):
        dst[...] = src[0].astype(BF16)


def _slab_rows(rows, steps):
    for r in range(BF16_ROWS_PER_VREG_V7X, rows + 1, BF16_ROWS_PER_VREG_V7X):
        if rows % r == 0 and rows // r <= steps:
            return r
    raise ValueError((rows, steps))


def _pre_call(h, g1, win, wo, gm, wm, wa, wgt, layer, later_weights):
    t, d = h.shape
    tm = TOKEN_TILE
    ng = wgt.shape[0]
    n_av = A_HEADS * A_DV
    n_vt = A_HEADS * (A_DV + BF16_ROWS_PER_VREG_V7X)
    steps = t // tm
    row = lambda n: pl.BlockSpec((tm, n), lambda i: (i, 0))
    weights = (g1, win, wo, gm, wm, wa, wgt)
    slabs_in, slabs_out = [], []
    for w in later_weights:
        r = _slab_rows(w.shape[1], steps)
        last = w.shape[1] // r - 1
        slabs_in.append(pl.BlockSpec((1, r, w.shape[2]), lambda i, last=last: (layer, jnp.minimum(i, last), 0)))
        slabs_out.append(pl.BlockSpec((r, w.shape[2]), lambda i, last=last: (jnp.minimum(i, last), 0)))
    outs = pl.pallas_call(
        _pre_kernel,
        grid=(steps,),
        in_specs=[row(d)] + [_resident(w.shape) for w in weights] + slabs_in,
        out_specs=[row(d), row(wm.shape[1]), row(wa.shape[1] - n_av),
                   pl.BlockSpec((n_vt, tm), lambda i: (0, i)), pl.BlockSpec((ng, tm), lambda i: (0, i))]
                  + slabs_out,
        out_shape=[jax.ShapeDtypeStruct((t, d), F32),
                   jax.ShapeDtypeStruct((t, wm.shape[1]), BF16),
                   jax.ShapeDtypeStruct((t, wa.shape[1] - n_av), BF16),
                   jax.ShapeDtypeStruct((n_vt, t), BF16),
                   jax.ShapeDtypeStruct((ng, t), F32)]
                  + [jax.ShapeDtypeStruct(w.shape[1:], BF16) for w in later_weights],
        compiler_params=pltpu.CompilerParams(
            dimension_semantics=("arbitrary",), vmem_limit_bytes=VMEM_LIMIT_BYTES_V7X),
        name="pre_ffn_inproj",
    )(h, *weights, *later_weights)
    return outs[:5], outs[5:]


def _post_kernel(h_ref, hf_ref, hb_ref, og_ref, ya_ref, p_ref, gn_ref, wout_ref, g2_ref, win_ref, wo_ref,
                 gp_ref, wpg_ref, wpp_ref, gf_ref, o_ref, *, final_norm):
    dm = hf_ref.shape[1]
    h = h_ref[...] + _dot(ya_ref[...], wout_ref[dm:, :])
    heads = []
    for c0 in range(0, dm, M_DV):
        hh = hf_ref[:, c0:c0 + M_DV] + hb_ref[:, c0:c0 + M_DV]
        hn = hh * lax.rsqrt(jnp.mean(hh * hh, axis=-1, keepdims=True) + EPS) * gn_ref[:, c0:c0 + M_DV]
        heads.append((jax.nn.sigmoid(og_ref[:, c0:c0 + M_DV].astype(F32)) * hn).astype(BF16))
    h = h + _dot(jnp.concatenate(heads, axis=1), wout_ref[0:dm, :])
    xn = _rmsnorm(h, g2_ref[...]).astype(BF16)
    h = h + 0.5 * _swiglu(xn, win_ref, wo_ref)
    gate = jax.nn.sigmoid(_dot(_rmsnorm(h, gp_ref[...]).astype(BF16), wpg_ref[...]))
    h = h + gate * _dot(p_ref[...].astype(BF16), wpp_ref[...])
    if final_norm:
        h = _rmsnorm(h, gf_ref[...])
    o_ref[...] = h


def _post_call(h1, hf, hb, zm, og_block, ya, p, gn, wout, g2, win, wo, gp, wpg, wpp, gf, final_norm):
    t, d = h1.shape
    tm = TOKEN_TILE
    row = lambda n: pl.BlockSpec((tm, n), lambda i: (i, 0))
    weights = (gn, wout, g2, win, wo, gp, wpg, wpp, gf)
    return pl.pallas_call(
        functools.partial(_post_kernel, final_norm=final_norm),
        grid=(t // tm,),
        in_specs=[row(d), row(hf.shape[1]), row(hb.shape[1]),
                  pl.BlockSpec((tm, hf.shape[1]), lambda i: (i, og_block)), row(ya.shape[1]), row(p.shape[1])]
                 + [_resident(w.shape) for w in weights],
        out_specs=row(d),
        out_shape=jax.ShapeDtypeStruct((t, d), F32),
        compiler_params=pltpu.CompilerParams(
            dimension_semantics=("arbitrary",), vmem_limit_bytes=VMEM_LIMIT_BYTES_V7X),
        name="post_outproj_ffn_ple",
    )(h1, hf, hb, zm, ya, p, *weights)


def _attn_kernel(lam_ref, slope_ref, g_ref, q_ref, k_ref, vt_ref, qn_ref, kn_ref, o_ref,
                 tab_s, st0_s, st1_s, m0_s, *, lam_init, tq, kblk):
    s = k_ref.shape[1]
    nq = s // tq
    nkb = s // kblk
    first_row = pl.program_id(1) == 0

    @pl.when(first_row)
    def _():
        r = lax.broadcasted_iota(jnp.int32, tab_s.shape, 1)
        c = lax.broadcasted_iota(jnp.int32, tab_s.shape, 0)
        tab_s[...] = jnp.abs(r - c + (s - tq)).astype(F32) * slope_ref[0, :, 0:1]

    lam_v = lam_ref[...]
    lam = (jnp.exp(jnp.sum(lam_v[0:1] * lam_v[1:2], axis=1, keepdims=True))
           - jnp.exp(jnp.sum(lam_v[2:3] * lam_v[3:4], axis=1, keepdims=True)) + lam_init)

    gain = g_ref[0] * (1.0 - lam_init)
    lane = lax.broadcasted_iota(jnp.int32, (tq, 2 * A_DK), 1)
    neg = jnp.full((1, 2 * tq), -jnp.inf, F32)
    acc0 = jnp.zeros((vt_ref.shape[0], 2 * tq), F32)

    def stage(tile, q, keys_ref, dst, m_prev, src):
        zero = jnp.zeros_like(q)
        qm = jnp.concatenate([jnp.where(lane < A_DK, q, zero), jnp.where(lane >= A_DK, q, zero)], axis=0)
        mx, acc = neg, acc0
        for kb in range(nkb):
            rows = slice(kb * kblk, (kb + 1) * kblk)
            bias = tab_s[pl.ds(_aligned((nq - 1 - tile) * tq + kb * kblk, kblk), kblk), :]
            sc = _dot_nt(keys_ref[0, rows, :], qm) - jnp.concatenate([bias, bias], axis=1)
            dst[rows, :] = sc
            mx = jnp.maximum(mx, jnp.max(sc, axis=0, keepdims=True))
            if src is not None:
                e = jnp.exp2(src[rows, :] - m_prev).astype(BF16)
                acc = acc + _dot(vt_ref[:, rows], e)
        return mx, acc

    def finish(u, acc):
        r = 1.0 / acc[A_DV:A_DV + 1, :]
        o = acc[:A_DV, :tq] * r[:, :tq] - lam * (acc[:A_DV, tq:] * r[:, tq:])
        on = o * lax.rsqrt(jnp.mean(o * o, axis=0, keepdims=True) + EPS)
        o_ref[0, pl.ds(_aligned(u * tq, tq), tq), :] = (on.T * gain).astype(o_ref.dtype)

    assert nq % 2 == 0 and nq >= 4

    def q_tile(u):
        return q_ref[0, pl.ds(_aligned(u * tq, tq), tq), :]

    @pl.when(first_row)
    def _():
        m0_s[...] = stage(0, q_tile(0), k_ref, st0_s, None, None)[0]

    def pair(j, carry, last):
        m_even, acc_odd = carry
        if acc_odd is not None:
            finish(2 * j - 1, acc_odd)
        m_odd, acc_even = stage(2 * j + 1, q_tile(2 * j + 1), k_ref, st1_s, m_even, st0_s)
        finish(2 * j, acc_even)
        if last:
            return stage(0, qn_ref[0], kn_ref, st0_s, m_odd, st1_s)
        return stage(2 * j + 2, q_tile(2 * j + 2), k_ref, st0_s, m_odd, st1_s)

    carry = pair(0, (m0_s[...], None), False)
    carry = lax.fori_loop(1, nq // 2 - 1, lambda j, c: pair(j, c, False), carry)
    m_next, acc_last = pair(nq // 2 - 1, carry, True)
    finish(nq - 1, acc_last)
    m0_s[...] = m_next


def _attn_call(za, vt, lam_vec, slopes, g_anorm, lam_init):
    b, s, _ = za.shape
    tq = ATTN_Q_TILE
    blk = 2 * A_DK
    assert blk == A_DV == LANES_V7X
    head_block = lambda off: pl.BlockSpec((1, s, blk), lambda h, i: (i, 0, off + h))
    nxt = lambda i: jnp.minimum(i + 1, b - 1)
    return pl.pallas_call(
        functools.partial(_attn_kernel, lam_init=lam_init, tq=tq, kblk=ATTN_KEY_BLOCK),
        grid=(A_HEADS, b),
        in_specs=[pl.BlockSpec(lam_vec.shape, lambda h, i: (0, 0)),
                  pl.BlockSpec((1, 1, LANES_V7X), lambda h, i: (h, 0, 0)),
                  pl.BlockSpec((1, 1, A_DV), lambda h, i: (h, 0, 0)),
                  head_block(0), head_block(A_HEADS),
                  pl.BlockSpec((A_DV + BF16_ROWS_PER_VREG_V7X, s), lambda h, i: (h, i)),
                  pl.BlockSpec((1, tq, blk), lambda h, i: (nxt(i), 0, h)),
                  pl.BlockSpec((1, s, blk), lambda h, i: (nxt(i), 0, A_HEADS + h))],
        out_specs=pl.BlockSpec((1, s, A_DV), lambda h, i: (i, 0, h)),
        out_shape=jax.ShapeDtypeStruct((b, s, A_HEADS * A_DV), BF16),
        scratch_shapes=[pltpu.VMEM((2 * s - tq, tq), F32),
                        pltpu.VMEM((s, 2 * tq), F32),
                        pltpu.VMEM((s, 2 * tq), F32),
                        pltpu.VMEM((1, 2 * tq), F32)],
        compiler_params=pltpu.CompilerParams(
            dimension_semantics=("arbitrary", "arbitrary"), vmem_limit_bytes=VMEM_LIMIT_BYTES_V7X),
        name="diff_attention",
    )(lam_vec, slopes, g_anorm, za, za, vt, za, za)


def _log_sigmoid(x):
    return jnp.minimum(x, 0.0) - jnp.log(1.0 + jnp.exp(-jnp.abs(x)))


def _chunk_scan(x, op, fill, pos_in_chunk, chunk, reverse):
    n = x.shape[1]
    k = 1
    while k < chunk:
        if reverse:
            shifted = pltpu.roll(x, n - k, 1)
            valid = pos_in_chunk < chunk - k
        else:
            shifted = pltpu.roll(x, k, 1)
            valid = pos_in_chunk >= k
        x = op(x, jnp.where(valid, shifted, fill))
        k *= 2
    return x


def _lane_replicated(row):
    return jnp.broadcast_to(row, (LANES_V7X, row.shape[1])).T


def _mlstm_kernel(zm_ref, zgt_ref, convw_ref, bg_ref, hf_ref, hb_ref,
                  qx_s, kt_s, ktb_s, brow_s, cmrow_s, crow_s, c_s, m_s, *, chunk, rblk):
    s = zm_ref.shape[1]
    nc = s // chunk
    nseq = 2 * M_HEADS
    dqk = M_HEADS * M_DK

    cw = convw_ref[...]
    lane_q = lax.broadcasted_iota(jnp.int32, (rblk, 2 * M_DK), 1)
    for j in range(s // rblk):
        r0 = j * rblk
        halo = SUBLANES_V7X
        parts = []
        if j == 0:
            parts.append(jnp.zeros((halo, 2 * dqk), F32))
        else:
            parts.append(zm_ref[0, r0 - halo:r0, 0:2 * dqk].astype(F32))
        parts.append(zm_ref[0, r0:r0 + rblk, 0:2 * dqk].astype(F32))
        if j == s // rblk - 1:
            parts.append(jnp.zeros((halo, 2 * dqk), F32))
        else:
            parts.append(zm_ref[0, r0 + rblk:r0 + rblk + halo, 0:2 * dqk].astype(F32))
        xe = jnp.concatenate(parts, axis=0)
        ne = rblk + 2 * halo
        y = (pltpu.roll(xe, 1, 0) * cw[0:1] + xe * cw[1:2] + pltpu.roll(xe, ne - 1, 0) * cw[2:3])
        y = y[halo:halo + rblk]
        y = y * jax.nn.sigmoid(y)
        q = y[:, :dqk].astype(BF16)
        kk = y[:, dqk:] * (M_DK ** -0.5)
        kt = kk.T
        kt_s[:, r0:r0 + rblk] = kt
        ktb_s[:, r0:r0 + rblk] = kt.astype(BF16)
        for h in range(M_HEADS):
            pair = q[:, (h // 2) * 2 * M_DK:(h // 2 + 1) * 2 * M_DK]
            keep = (lane_q >= M_DK) if (h % 2) else (lane_q < M_DK)
            qx_s[r0:r0 + rblk, h * 2 * M_DK:(h + 1) * 2 * M_DK] = jnp.where(keep, pair, jnp.zeros_like(pair))

    g = zgt_ref[...] + bg_ref[...]
    li = g[0:nseq] * LOG2E
    lf = _log_sigmoid(g[nseq:2 * nseq]) * LOG2E
    pos = lax.broadcasted_iota(jnp.int32, li.shape, 1) % chunk
    is_fw = lax.broadcasted_iota(jnp.int32, li.shape, 0) < M_HEADS
    b = jnp.where(is_fw, _chunk_scan(lf, jnp.add, 0.0, pos, chunk, False),
                  _chunk_scan(lf, jnp.add, 0.0, pos, chunk, True))
    c = li - b
    cmax = jnp.where(is_fw, _chunk_scan(c, jnp.maximum, -jnp.inf, pos, chunk, False),
                     _chunk_scan(c, jnp.maximum, -jnp.inf, pos, chunk, True))
    for j in range(nc):
        cols = slice(j * chunk, (j + 1) * chunk)
        brow_s[j] = b[:, cols]
        cmrow_s[j] = cmax[:, cols]
        crow_s[j] = c[:, cols]

    c_s[...] = jnp.zeros_like(c_s)
    m_s[...] = jnp.zeros_like(m_s)

    rr = lax.broadcasted_iota(jnp.int32, (chunk, chunk), 0)
    cc = lax.broadcasted_iota(jnp.int32, (chunk, chunk), 1)
    causal = (cc <= rr, cc >= rr)
    ones = jnp.ones((chunk, M_DV), BF16)

    def step(j, carry):
        for d in range(2):
            cj = j if d == 0 else nc - 1 - j
            rows = pl.ds(pl.multiple_of(cj * chunk, chunk), chunk)
            last = chunk - 1 if d == 0 else 0
            h_out = hf_ref if d == 0 else hb_ref
            for pr in range(M_HEADS // 2):
                state = c_s[d, pr]
                state_bf = state.astype(BF16)
                ktp = ktb_s[pr * 2 * M_DK:(pr + 1) * 2 * M_DK, rows]
                for half in range(2):
                    h = 2 * pr + half
                    i = d * M_HEADS + h
                    blk = slice(h * M_DV, (h + 1) * M_DV)
                    qc = qx_s[rows, blk]
                    v_aug = jnp.concatenate(
                        [zm_ref[0, rows, 2 * dqk + h * M_DV:2 * dqk + (h + 1) * M_DV], ones], axis=1)
                    crow = crow_s[cj, i:i + 1, :]
                    m_old = m_s[i:i + 1, :]
                    b_rep = _lane_replicated(brow_s[cj, i:i + 1, :])
                    m_rep = jnp.maximum(m_old, _lane_replicated(cmrow_s[cj, i:i + 1, :]))
                    dt = jnp.where(causal[d], jnp.exp2(crow - m_rep), 0.0)
                    sm = (_dot(qc, ktp) * dt).astype(BF16)
                    qw = (qc * jnp.exp2(m_old - m_rep)).astype(BF16)
                    tot = _dot(jnp.concatenate([qw, sm], axis=1),
                               jnp.concatenate([state_bf, v_aug], axis=0))
                    floor = jnp.exp2(-(b_rep + m_rep))
                    h_out[0, rows, blk] = tot[:, :M_DV] / jnp.maximum(jnp.abs(tot[:, M_DV:]), floor)
                    m_last = m_rep[last:last + 1, :]
                    wk = jnp.exp2(crow - m_last[:, 0:1])
                    kth = kt_s[h * M_DK:(h + 1) * M_DK, rows]
                    dc = _dot((kth * wk).astype(BF16), v_aug)
                    sub = slice(half * M_DK, (half + 1) * M_DK)
                    c_s[d, pr, sub, :] = jnp.exp2(m_old - m_last)[:, 0:1] * state[sub] + dc
                    m_s[i:i + 1, :] = b_rep[last:last + 1, :] + m_last
        return carry

    def steps(jj, carry):
        for k in range(MLSTM_CHUNKS_PER_STEP):
            step(jj * MLSTM_CHUNKS_PER_STEP + k, carry)
        return carry

    lax.fori_loop(0, nc // MLSTM_CHUNKS_PER_STEP, steps, 0)


def _mlstm_call(zm, zgt, conv_w, bg):
    b, s, wz = zm.shape
    chunk = MLSTM_CHUNK
    nc = s // chunk
    dv = M_HEADS * M_DV
    dqk = M_HEADS * M_DK
    nseq = 2 * M_HEADS
    full = lambda a: pl.BlockSpec(a.shape, lambda i: (0,) * a.ndim)
    per_b = lambda n: pl.BlockSpec((1, s, n), lambda i: (i, 0, 0))
    gate_rows = lambda: pltpu.VMEM((nc, nseq, chunk), F32)
    return pl.pallas_call(
        functools.partial(_mlstm_kernel, chunk=chunk, rblk=MLSTM_ROW_BLOCK),
        grid=(b,),
        in_specs=[per_b(wz), pl.BlockSpec((zgt.shape[0], s), lambda i: (0, i)), full(conv_w), full(bg)],
        out_specs=[per_b(dv), per_b(dv)],
        out_shape=[jax.ShapeDtypeStruct((b, s, dv), F32), jax.ShapeDtypeStruct((b, s, dv), F32)],
        scratch_shapes=[
            pltpu.VMEM((s, M_HEADS * 2 * M_DK), BF16),
            pltpu.VMEM((dqk, s), F32),
            pltpu.VMEM((dqk, s), BF16),
            gate_rows(),
            gate_rows(),
            gate_rows(),
            pltpu.VMEM((2, M_HEADS // 2, 2 * M_DK, 2 * M_DV), F32),
            pltpu.VMEM((nseq, LANES_V7X), F32),
        ],
        compiler_params=pltpu.CompilerParams(
            dimension_semantics=("arbitrary",), vmem_limit_bytes=VMEM_LIMIT_BYTES_V7X),
        name="mlstm_bidir",
    )(zm, zgt, conv_w, bg)


def _gate_rows(a16):
    hh = M_HEADS
    return jnp.concatenate([a16[..., 0:hh], a16[..., 2 * hh:3 * hh], a16[..., hh:2 * hh], a16[..., 3 * hh:]],
                           axis=-1)


def kernel(x, p, g_ffn1, w_ffn1_in, w_ffn1_out, g_mix, w_in, b_mgate, conv_w, g_mnorm, lam_q1, lam_k1, lam_q2, lam_k2, g_anorm, w_out, g_ffn2, w_ffn2_in, w_ffn2_out, g_ple, w_ple_gate, w_ple_proj, g_final):
    bsz, s, d = x.shape
    depth = w_in.shape[0]
    t = bsz * s
    m_qk, m_v, m_g = M_HEADS * M_DK, M_HEADS * M_DV, 4 * M_HEADS
    a_qk = A_HEADS * 2 * A_DK
    m_end = 2 * m_qk + 2 * m_v
    assert w_in.shape[2] == m_end + m_g + 2 * a_qk + A_HEADS * A_DV and conv_w.shape[1:] == (CONV_K, 2 * m_qk)
    assert t % TOKEN_TILE == 0 and w_ffn1_out.shape[1] % FF_CHUNK == 0 and w_ffn2_out.shape[1] % FF_CHUNK == 0
    assert s % (MLSTM_CHUNK * MLSTM_CHUNKS_PER_STEP) == 0 and s % MLSTM_ROW_BLOCK == 0
    assert s % ATTN_Q_TILE == 0 and ATTN_Q_TILE % ATTN_KEY_BLOCK == 0
    slopes = np.array([2.0 ** (-8.0 * (h + 1) / A_HEADS) for h in range(A_HEADS)], dtype=np.float32)
    slopes = jnp.broadcast_to(jnp.asarray(slopes * np.float32(LOG2E))[:, None, None],
                              (A_HEADS, 1, LANES_V7X))
    row = lambda v: v.reshape(1, -1).astype(F32)

    h = x.reshape(t, d)
    for i in range(depth):
        lam_init = 0.8 - 0.6 * math.exp(-0.3 * i)
        wi = w_in[i]
        wm = wi[:, :m_end].astype(BF16)
        wa = wi[:, m_end + m_g:]
        wa = jnp.concatenate([wa[:, :a_qk] * (A_DK ** -0.5 * LOG2E), wa[:, a_qk:]], axis=1).astype(BF16)
        wgt = _gate_rows(wi[:, m_end:m_end + m_g]).T.astype(BF16)
        bg = _gate_rows(b_mgate[i].astype(F32)).reshape(m_g, 1)

        (h1, zm, za, vt, zgt), (win2, wo2, wout, wpg, wpp) = _pre_call(
            h, row(g_ffn1[i]), w_ffn1_in[i].astype(BF16), w_ffn1_out[i].astype(BF16), row(g_mix[i]), wm, wa, wgt,
            i, (w_ffn2_in, w_ffn2_out, w_out, w_ple_gate, w_ple_proj))
        hf, hb = _mlstm_call(zm.reshape(bsz, s, -1), zgt, conv_w[i].astype(F32), bg)
        lam_vec = jnp.stack([lam_q1[i], lam_k1[i], lam_q2[i], lam_k2[i]]).astype(F32)
        ya = _attn_call(za.reshape(bsz, s, -1), vt, lam_vec, slopes,
                        g_anorm[i].reshape(A_HEADS, 1, A_DV).astype(F32), lam_init)
        assert (2 * m_qk + m_v) % m_v == 0
        h = _post_call(h1, hf.reshape(t, -1), hb.reshape(t, -1), zm, (2 * m_qk + m_v) // m_v, ya.reshape(t, -1),
                       p[i].reshape(t, -1), row(g_mnorm[i]),
                       wout, row(g_ffn2[i]), win2, wo2, row(g_ple[i]), wpg, wpp, row(g_final),
                       final_norm=(i == depth - 1))
    return h.reshape(bsz, s, d)
```

```python
import functools
import math

import jax
import jax.numpy as jnp
import numpy as np
from jax import lax
from jax.experimental import pallas as pl
from jax.experimental.pallas import tpu as pltpu

F32 = jnp.float32
BF16 = jnp.bfloat16

EPS = 1e-6
LOG2E = 1.4426950408889634

M_HEADS = 4
M_DK = 64
M_DV = 128
A_HEADS = 4
A_DK = 64
A_DV = 128
CONV_K = 3

LANES_V7X = 128
SUBLANES_V7X = 8
BF16_ROWS_PER_VREG_V7X = 16
VMEM_LIMIT_BYTES_V7X = 56 * 1024 * 1024

TOKEN_TILE = 512
FF_CHUNK = 256
MLSTM_CHUNK = 128
MLSTM_CHUNKS_PER_STEP = 8
MLSTM_ROW_BLOCK = 128
ATTN_Q_TILE = 256
ATTN_KEY_BLOCK = 256
ATTN_LOGIT_SPARE_LANES = 2 * LANES_V7X


def _aligned(x, m):
    return x if isinstance(x, int) else pl.multiple_of(x, m)


def _dot(a, b):
    return jnp.dot(a, b, preferred_element_type=F32)


def _dot_nt(a, b):
    return lax.dot_general(a, b, (((1,), (1,)), ((), ())), preferred_element_type=F32)


def _rmsnorm(x, g):
    ms = jnp.mean(x * x, axis=-1, keepdims=True)
    return x * lax.rsqrt(ms + EPS) * g


def _swiglu(xn, win_ref, wo_ref):
    f = wo_ref.shape[0]
    parts = []
    for c0 in range(0, f, FF_CHUNK):
        g = _dot(xn, win_ref[:, c0:c0 + FF_CHUNK])
        u = _dot(xn, win_ref[:, f + c0:f + c0 + FF_CHUNK])
        parts.append((g * jax.nn.sigmoid(g) * u).astype(BF16))
    return _dot(jnp.concatenate(parts, axis=1), wo_ref[...])


def _resident(shape):
    nd = len(shape)
    return pl.BlockSpec(shape, lambda *_: (0,) * nd, pipeline_mode=pl.Buffered(1))


def _pre_kernel(x_ref, g1_ref, win_ref, wo_ref, gm_ref, wm_ref, wa_ref, wgt_ref, *rest):
    n_cast = (len(rest) - 5) // 2
    cast_in, (h_ref, zm_ref, za_ref, vt_ref, zgt_ref), cast_out = (
        rest[:n_cast], rest[n_cast:n_cast + 5], rest[n_cast + 5:])
    x = x_ref[...]
    xn = _rmsnorm(x, g1_ref[...]).astype(BF16)
    h = x + 0.5 * _swiglu(xn, win_ref, wo_ref)
    h_ref[...] = h
    u = _rmsnorm(h, gm_ref[...]).astype(BF16)
    zm_ref[...] = _dot(u, wm_ref[...]).astype(BF16)
    za = _dot(u, wa_ref[...])
    nqk = za_ref.shape[1]
    za_ref[...] = za[:, :nqk].astype(BF16)
    vt = za[:, nqk:].T.astype(BF16)
    one = jnp.ones((BF16_ROWS_PER_VREG_V7X, vt.shape[1]), BF16)
    vt_ref[...] = jnp.concatenate(
        [blk for c0 in range(0, vt.shape[0], A_DV) for blk in (vt[c0:c0 + A_DV], one)], axis=0)
    zgt_ref[...] = _dot_nt(wgt_ref[...], u)
    for src, dst in zip(cast_in, cast_out):
        dst[...] = src[0].astype(BF16)


def _slab_rows(rows, steps):
    for r in range(BF16_ROWS_PER_VREG_V7X, rows + 1, BF16_ROWS_PER_VREG_V7X):
        if rows % r == 0 and rows // r <= steps:
            return r
    raise ValueError((rows, steps))


def _pre_call(h, g1, win, wo, gm, wm, wa, wgt, layer, later_weights):
    t, d = h.shape
    tm = TOKEN_TILE
    ng = wgt.shape[0]
    n_av = A_HEADS * A_DV
    n_vt = A_HEADS * (A_DV + BF16_ROWS_PER_VREG_V7X)
    steps = t // tm
    row = lambda n: pl.BlockSpec((tm, n), lambda i: (i, 0))
    weights = (g1, win, wo, gm, wm, wa, wgt)
    slabs_in, slabs_out = [], []
    for w in later_weights:
        r = _slab_rows(w.shape[1], steps)
        last = w.shape[1] // r - 1
        slabs_in.append(pl.BlockSpec((1, r, w.shape[2]), lambda i, last=last: (layer, jnp.minimum(i, last), 0)))
        slabs_out.append(pl.BlockSpec((r, w.shape[2]), lambda i, last=last: (jnp.minimum(i, last), 0)))
    outs = pl.pallas_call(
        _pre_kernel,
        grid=(steps,),
        in_specs=[row(d)] + [_resident(w.shape) for w in weights] + slabs_in,
        out_specs=[row(d), row(wm.shape[1]), row(wa.shape[1] - n_av),
                   pl.BlockSpec((n_vt, tm), lambda i: (0, i)), pl.BlockSpec((ng, tm), lambda i: (0, i))]
                  + slabs_out,
        out_shape=[jax.ShapeDtypeStruct((t, d), F32),
                   jax.ShapeDtypeStruct((t, wm.shape[1]), BF16),
                   jax.ShapeDtypeStruct((t, wa.shape[1] - n_av), BF16),
                   jax.ShapeDtypeStruct((n_vt, t), BF16),
                   jax.ShapeDtypeStruct((ng, t), F32)]
                  + [jax.ShapeDtypeStruct(w.shape[1:], BF16) for w in later_weights],
        compiler_params=pltpu.CompilerParams(
            dimension_semantics=("arbitrary",), vmem_limit_bytes=VMEM_LIMIT_BYTES_V7X),
        name="pre_ffn_inproj",
    )(h, *weights, *later_weights)
    return outs[:5], outs[5:]


def _post_kernel(h_ref, hf_ref, hb_ref, og_ref, ya_ref, p_ref, gn_ref, wout_ref, g2_ref, win_ref, wo_ref,
                 gp_ref, wpg_ref, wpp_ref, gf_ref, o_ref, *, final_norm):
    dm = hf_ref.shape[1]
    h = h_ref[...] + _dot(ya_ref[...], wout_ref[dm:, :])
    heads = []
    for c0 in range(0, dm, M_DV):
        hh = hf_ref[:, c0:c0 + M_DV] + hb_ref[:, c0:c0 + M_DV]
        hn = hh * lax.rsqrt(jnp.mean(hh * hh, axis=-1, keepdims=True) + EPS) * gn_ref[:, c0:c0 + M_DV]
        heads.append((jax.nn.sigmoid(og_ref[:, c0:c0 + M_DV].astype(F32)) * hn).astype(BF16))
    h = h + _dot(jnp.concatenate(heads, axis=1), wout_ref[0:dm, :])
    xn = _rmsnorm(h, g2_ref[...]).astype(BF16)
    h = h + 0.5 * _swiglu(xn, win_ref, wo_ref)
    gate = jax.nn.sigmoid(_dot(_rmsnorm(h, gp_ref[...]).astype(BF16), wpg_ref[...]))
    h = h + gate * _dot(p_ref[...].astype(BF16), wpp_ref[...])
    if final_norm:
        h = _rmsnorm(h, gf_ref[...])
    o_ref[...] = h


def _post_call(h1, hf, hb, zm, og_block, ya, p, gn, wout, g2, win, wo, gp, wpg, wpp, gf, final_norm):
    t, d = h1.shape
    tm = TOKEN_TILE
    row = lambda n: pl.BlockSpec((tm, n), lambda i: (i, 0))
    weights = (gn, wout, g2, win, wo, gp, wpg, wpp, gf)
    return pl.pallas_call(
        functools.partial(_post_kernel, final_norm=final_norm),
        grid=(t // tm,),
        in_specs=[row(d), row(hf.shape[1]), row(hb.shape[1]),
                  pl.BlockSpec((tm, hf.shape[1]), lambda i: (i, og_block)), row(ya.shape[1]), row(p.shape[1])]
                 + [_resident(w.shape) for w in weights],
        out_specs=row(d),
        out_shape=jax.ShapeDtypeStruct((t, d), F32),
        compiler_params=pltpu.CompilerParams(
            dimension_semantics=("arbitrary",), vmem_limit_bytes=VMEM_LIMIT_BYTES_V7X),
        name="post_outproj_ffn_ple",
    )(h1, hf, hb, zm, ya, p, *weights)


def _attn_kernel(lam_ref, slope_ref, g_ref, q_ref, k_ref, vt_ref, qn_ref, kn_ref, o_ref,
                 tab_s, st0_s, st1_s, m0_s, *, lam_init, tq, kblk):
    s = k_ref.shape[1]
    nq = s // tq
    nkb = s // kblk
    first_row = pl.program_id(1) == 0

    @pl.when(first_row)
    def _():
        r = lax.broadcasted_iota(jnp.int32, tab_s.shape, 1)
        c = lax.broadcasted_iota(jnp.int32, tab_s.shape, 0)
        tab_s[...] = jnp.abs(r - c + (s - tq)).astype(F32) * slope_ref[0, :, 0:1]

    lam_v = lam_ref[...]
    lam = (jnp.exp(jnp.sum(lam_v[0:1] * lam_v[1:2], axis=1, keepdims=True))
           - jnp.exp(jnp.sum(lam_v[2:3] * lam_v[3:4], axis=1, keepdims=True)) + lam_init)

    gain = g_ref[0] * (1.0 - lam_init)
    lane = lax.broadcasted_iota(jnp.int32, (tq, 2 * A_DK), 1)
    neg = jnp.full((1, 2 * tq), -jnp.inf, F32)
    acc0 = jnp.zeros((vt_ref.shape[0], 2 * tq), F32)

    def stage(tile, q, keys_ref, dst, m_prev, src):
        zero = jnp.zeros_like(q)
        qm = jnp.concatenate([jnp.where(lane < A_DK, q, zero), jnp.where(lane >= A_DK, q, zero)], axis=0)
        mx, acc = neg, acc0
        for kb in range(nkb):
            rows = slice(kb * kblk, (kb + 1) * kblk)
            bias = tab_s[pl.ds(_aligned((nq - 1 - tile) * tq + kb * kblk, kblk), kblk), :]
            sc = _dot_nt(keys_ref[0, rows, :], qm) - jnp.concatenate([bias, bias], axis=1)
            dst[rows, 0:2 * tq] = sc
            mx = jnp.maximum(mx, jnp.max(sc, axis=0, keepdims=True))
            if src is not None:
                e = jnp.exp2(src[rows, 0:2 * tq] - m_prev).astype(BF16)
                acc = acc + _dot(vt_ref[:, rows], e)
        return mx, acc

    def finish(u, acc):
        r = 1.0 / acc[A_DV:A_DV + 1, :]
        o = acc[:A_DV, :tq] * r[:, :tq] - lam * (acc[:A_DV, tq:] * r[:, tq:])
        on = o * lax.rsqrt(jnp.mean(o * o, axis=0, keepdims=True) + EPS)
        o_ref[0, pl.ds(_aligned(u * tq, tq), tq), :] = (on.T * gain).astype(o_ref.dtype)

    assert nq % 2 == 0 and nq >= 4

    def q_tile(u):
        return q_ref[0, pl.ds(_aligned(u * tq, tq), tq), :]

    @pl.when(first_row)
    def _():
        m0_s[...] = stage(0, q_tile(0), k_ref, st0_s, None, None)[0]

    def pair(j, carry, last):
        m_even, acc_odd = carry
        if acc_odd is not None:
            finish(2 * j - 1, acc_odd)
        m_odd, acc_even = stage(2 * j + 1, q_tile(2 * j + 1), k_ref, st1_s, m_even, st0_s)
        finish(2 * j, acc_even)
        if last:
            return stage(0, qn_ref[0], kn_ref, st0_s, m_odd, st1_s)
        return stage(2 * j + 2, q_tile(2 * j + 2), k_ref, st0_s, m_odd, st1_s)

    carry = pair(0, (m0_s[...], None), False)
    carry = lax.fori_loop(1, nq // 2 - 1, lambda j, c: pair(j, c, False), carry)
    m_next, acc_last = pair(nq // 2 - 1, carry, True)
    finish(nq - 1, acc_last)
    m0_s[...] = m_next


def _attn_call(za, vt, lam_vec, slopes, g_anorm, lam_init):
    b, s, _ = za.shape
    tq = ATTN_Q_TILE
    blk = 2 * A_DK
    assert blk == A_DV == LANES_V7X
    head_block = lambda off: pl.BlockSpec((1, s, blk), lambda h, i: (i, 0, off + h))
    nxt = lambda i: jnp.minimum(i + 1, b - 1)
    logits = lambda: pltpu.VMEM((s, 2 * tq + ATTN_LOGIT_SPARE_LANES), F32)
    return pl.pallas_call(
        functools.partial(_attn_kernel, lam_init=lam_init, tq=tq, kblk=ATTN_KEY_BLOCK),
        grid=(A_HEADS, b),
        in_specs=[pl.BlockSpec(lam_vec.shape, lambda h, i: (0, 0)),
                  pl.BlockSpec((1, 1, LANES_V7X), lambda h, i: (h, 0, 0)),
                  pl.BlockSpec((1, 1, A_DV), lambda h, i: (h, 0, 0)),
                  head_block(0), head_block(A_HEADS),
                  pl.BlockSpec((A_DV + BF16_ROWS_PER_VREG_V7X, s), lambda h, i: (h, i)),
                  pl.BlockSpec((1, tq, blk), lambda h, i: (nxt(i), 0, h)),
                  pl.BlockSpec((1, s, blk), lambda h, i: (nxt(i), 0, A_HEADS + h))],
        out_specs=pl.BlockSpec((1, s, A_DV), lambda h, i: (i, 0, h)),
        out_shape=jax.ShapeDtypeStruct((b, s, A_HEADS * A_DV), BF16),
        scratch_shapes=[pltpu.VMEM((2 * s - tq, tq), F32),
                        logits(),
                        logits(),
                        pltpu.VMEM((1, 2 * tq), F32)],
        compiler_params=pltpu.CompilerParams(
            dimension_semantics=("arbitrary", "arbitrary"), vmem_limit_bytes=VMEM_LIMIT_BYTES_V7X),
        name="diff_attention",
    )(lam_vec, slopes, g_anorm, za, za, vt, za, za)


def _log_sigmoid(x):
    return jnp.minimum(x, 0.0) - jnp.log(1.0 + jnp.exp(-jnp.abs(x)))


def _chunk_scan(x, op, fill, pos_in_chunk, chunk, reverse):
    n = x.shape[1]
    k = 1
    while k < chunk:
        if reverse:
            shifted = pltpu.roll(x, n - k, 1)
            valid = pos_in_chunk < chunk - k
        else:
            shifted = pltpu.roll(x, k, 1)
            valid = pos_in_chunk >= k
        x = op(x, jnp.where(valid, shifted, fill))
        k *= 2
    return x


def _lane_replicated(row):
    return jnp.broadcast_to(row, (LANES_V7X, row.shape[1])).T


def _mlstm_kernel(zm_ref, zgt_ref, convw_ref, bg_ref, hf_ref, hb_ref,
                  qx_s, kt_s, ktb_s, brow_s, cmrow_s, crow_s, c_s, m_s, *, chunk, rblk):
    s = zm_ref.shape[1]
    nc = s // chunk
    nseq = 2 * M_HEADS
    dqk = M_HEADS * M_DK

    cw = convw_ref[...]
    lane_q = lax.broadcasted_iota(jnp.int32, (rblk, 2 * M_DK), 1)
    for j in range(s // rblk):
        r0 = j * rblk
        halo = SUBLANES_V7X
        parts = []
        if j == 0:
            parts.append(jnp.zeros((halo, 2 * dqk), F32))
        else:
            parts.append(zm_ref[0, r0 - halo:r0, 0:2 * dqk].astype(F32))
        parts.append(zm_ref[0, r0:r0 + rblk, 0:2 * dqk].astype(F32))
        if j == s // rblk - 1:
            parts.append(jnp.zeros((halo, 2 * dqk), F32))
        else:
            parts.append(zm_ref[0, r0 + rblk:r0 + rblk + halo, 0:2 * dqk].astype(F32))
        xe = jnp.concatenate(parts, axis=0)
        ne = rblk + 2 * halo
        y = (pltpu.roll(xe, 1, 0) * cw[0:1] + xe * cw[1:2] + pltpu.roll(xe, ne - 1, 0) * cw[2:3])
        y = y[halo:halo + rblk]
        y = y * jax.nn.sigmoid(y)
        q = y[:, :dqk].astype(BF16)
        kk = y[:, dqk:] * (M_DK ** -0.5)
        kt = kk.T
        kt_s[:, r0:r0 + rblk] = kt
        ktb_s[:, r0:r0 + rblk] = kt.astype(BF16)
        for h in range(M_HEADS):
            pair = q[:, (h // 2) * 2 * M_DK:(h // 2 + 1) * 2 * M_DK]
            keep = (lane_q >= M_DK) if (h % 2) else (lane_q < M_DK)
            qx_s[r0:r0 + rblk, h * 2 * M_DK:(h + 1) * 2 * M_DK] = jnp.where(keep, pair, jnp.zeros_like(pair))

    g = zgt_ref[...] + bg_ref[...]
    li = g[0:nseq] * LOG2E
    lf = _log_sigmoid(g[nseq:2 * nseq]) * LOG2E
    pos = lax.broadcasted_iota(jnp.int32, li.shape, 1) % chunk
    is_fw = lax.broadcasted_iota(jnp.int32, li.shape, 0) < M_HEADS
    b = jnp.where(is_fw, _chunk_scan(lf, jnp.add, 0.0, pos, chunk, False),
                  _chunk_scan(lf, jnp.add, 0.0, pos, chunk, True))
    c = li - b
    cmax = jnp.where(is_fw, _chunk_scan(c, jnp.maximum, -jnp.inf, pos, chunk, False),
                     _chunk_scan(c, jnp.maximum, -jnp.inf, pos, chunk, True))
    for j in range(nc):
        cols = slice(j * chunk, (j + 1) * chunk)
        brow_s[j] = b[:, cols]
        cmrow_s[j] = cmax[:, cols]
        crow_s[j] = c[:, cols]

    c_s[...] = jnp.zeros_like(c_s)
    m_s[...] = jnp.zeros_like(m_s)

    rr = lax.broadcasted_iota(jnp.int32, (chunk, chunk), 0)
    cc = lax.broadcasted_iota(jnp.int32, (chunk, chunk), 1)
    causal = (cc <= rr, cc >= rr)
    ones = jnp.ones((chunk, M_DV), BF16)

    def step(j, carry):
        for d in range(2):
            cj = j if d == 0 else nc - 1 - j
            rows = pl.ds(pl.multiple_of(cj * chunk, chunk), chunk)
            last = chunk - 1 if d == 0 else 0
            h_out = hf_ref if d == 0 else hb_ref
            for pr in range(M_HEADS // 2):
                state = c_s[d, pr]
                state_bf = state.astype(BF16)
                ktp = ktb_s[pr * 2 * M_DK:(pr + 1) * 2 * M_DK, rows]
                for half in range(2):
                    h = 2 * pr + half
                    i = d * M_HEADS + h
                    blk = slice(h * M_DV, (h + 1) * M_DV)
                    qc = qx_s[rows, blk]
                    v_aug = jnp.concatenate(
                        [zm_ref[0, rows, 2 * dqk + h * M_DV:2 * dqk + (h + 1) * M_DV], ones], axis=1)
                    crow = crow_s[cj, i:i + 1, :]
                    m_old = m_s[i:i + 1, :]
                    b_rep = _lane_replicated(brow_s[cj, i:i + 1, :])
                    m_rep = jnp.maximum(m_old, _lane_replicated(cmrow_s[cj, i:i + 1, :]))
                    dt = jnp.where(causal[d], jnp.exp2(crow - m_rep), 0.0)
                    sm = (_dot(qc, ktp) * dt).astype(BF16)
                    qw = (qc * jnp.exp2(m_old - m_rep)).astype(BF16)
                    tot = _dot(jnp.concatenate([qw, sm], axis=1),
                               jnp.concatenate([state_bf, v_aug], axis=0))
                    floor = jnp.exp2(-(b_rep + m_rep))
                    h_out[0, rows, blk] = tot[:, :M_DV] / jnp.maximum(jnp.abs(tot[:, M_DV:]), floor)
                    m_last = m_rep[last:last + 1, :]
                    wk = jnp.exp2(crow - m_last[:, 0:1])
                    kth = kt_s[h * M_DK:(h + 1) * M_DK, rows]
                    dc = _dot((kth * wk).astype(BF16), v_aug)
                    sub = slice(half * M_DK, (half + 1) * M_DK)
                    c_s[d, pr, sub, :] = jnp.exp2(m_old - m_last)[:, 0:1] * state[sub] + dc
                    m_s[i:i + 1, :] = b_rep[last:last + 1, :] + m_last
        return carry

    def steps(jj, carry):
        for k in range(MLSTM_CHUNKS_PER_STEP):
            step(jj * MLSTM_CHUNKS_PER_STEP + k, carry)
        return carry

    lax.fori_loop(0, nc // MLSTM_CHUNKS_PER_STEP, steps, 0)


def _mlstm_call(zm, zgt, conv_w, bg):
    b, s, wz = zm.shape
    chunk = MLSTM_CHUNK
    nc = s // chunk
    dv = M_HEADS * M_DV
    dqk = M_HEADS * M_DK
    nseq = 2 * M_HEADS
    full = lambda a: pl.BlockSpec(a.shape, lambda i: (0,) * a.ndim)
    per_b = lambda n: pl.BlockSpec((1, s, n), lambda i: (i, 0, 0))
    gate_rows = lambda: pltpu.VMEM((nc, nseq, chunk), F32)
    return pl.pallas_call(
        functools.partial(_mlstm_kernel, chunk=chunk, rblk=MLSTM_ROW_BLOCK),
        grid=(b,),
        in_specs=[per_b(wz), pl.BlockSpec((zgt.shape[0], s), lambda i: (0, i)), full(conv_w), full(bg)],
        out_specs=[per_b(dv), per_b(dv)],
        out_shape=[jax.ShapeDtypeStruct((b, s, dv), F32), jax.ShapeDtypeStruct((b, s, dv), F32)],
        scratch_shapes=[
            pltpu.VMEM((s, M_HEADS * 2 * M_DK), BF16),
            pltpu.VMEM((dqk, s), F32),
            pltpu.VMEM((dqk, s), BF16),
            gate_rows(),
            gate_rows(),
            gate_rows(),
            pltpu.VMEM((2, M_HEADS // 2, 2 * M_DK, 2 * M_DV), F32),
            pltpu.VMEM((nseq, LANES_V7X), F32),
        ],
        compiler_params=pltpu.CompilerParams(
            dimension_semantics=("arbitrary",), vmem_limit_bytes=VMEM_LIMIT_BYTES_V7X),
        name="mlstm_bidir",
    )(zm, zgt, conv_w, bg)


def _gate_rows(a16):
    hh = M_HEADS
    return jnp.concatenate([a16[..., 0:hh], a16[..., 2 * hh:3 * hh], a16[..., hh:2 * hh], a16[..., 3 * hh:]],
                           axis=-1)


def kernel(x, p, g_ffn1, w_ffn1_in, w_ffn1_out, g_mix, w_in, b_mgate, conv_w, g_mnorm, lam_q1, lam_k1, lam_q2, lam_k2, g_anorm, w_out, g_ffn2, w_ffn2_in, w_ffn2_out, g_ple, w_ple_gate, w_ple_proj, g_final):
    bsz, s, d = x.shape
    depth = w_in.shape[0]
    t = bsz * s
    m_qk, m_v, m_g = M_HEADS * M_DK, M_HEADS * M_DV, 4 * M_HEADS
    a_qk = A_HEADS * 2 * A_DK
    m_end = 2 * m_qk + 2 * m_v
    assert w_in.shape[2] == m_end + m_g + 2 * a_qk + A_HEADS * A_DV and conv_w.shape[1:] == (CONV_K, 2 * m_qk)
    assert t % TOKEN_TILE == 0 and w_ffn1_out.shape[1] % FF_CHUNK == 0 and w_ffn2_out.shape[1] % FF_CHUNK == 0
    assert s % (MLSTM_CHUNK * MLSTM_CHUNKS_PER_STEP) == 0 and s % MLSTM_ROW_BLOCK == 0
    assert s % ATTN_Q_TILE == 0 and ATTN_Q_TILE % ATTN_KEY_BLOCK == 0
    slopes = np.array([2.0 ** (-8.0 * (h + 1) / A_HEADS) for h in range(A_HEADS)], dtype=np.float32)
    slopes = jnp.broadcast_to(jnp.asarray(slopes * np.float32(LOG2E))[:, None, None],
                              (A_HEADS, 1, LANES_V7X))
    row = lambda v: v.reshape(1, -1).astype(F32)

    h = x.reshape(t, d)
    for i in range(depth):
        lam_init = 0.8 - 0.6 * math.exp(-0.3 * i)
        wi = w_in[i]
        wm = wi[:, :m_end].astype(BF16)
        wa = wi[:, m_end + m_g:]
        wa = jnp.concatenate([wa[:, :a_qk] * (A_DK ** -0.5 * LOG2E), wa[:, a_qk:]], axis=1).astype(BF16)
        wgt = _gate_rows(wi[:, m_end:m_end + m_g]).T.astype(BF16)
        bg = _gate_rows(b_mgate[i].astype(F32)).reshape(m_g, 1)

        (h1, zm, za, vt, zgt), (win2, wo2, wout, wpg, wpp) = _pre_call(
            h, row(g_ffn1[i]), w_ffn1_in[i].astype(BF16), w_ffn1_out[i].astype(BF16), row(g_mix[i]), wm, wa, wgt,
            i, (w_ffn2_in, w_ffn2_out, w_out, w_ple_gate, w_ple_proj))
        hf, hb = _mlstm_call(zm.reshape(bsz, s, -1), zgt, conv_w[i].astype(F32), bg)
        lam_vec = jnp.stack([lam_q1[i], lam_k1[i], lam_q2[i], lam_k2[i]]).astype(F32)
        ya = _attn_call(za.reshape(bsz, s, -1), vt, lam_vec, slopes,
                        g_anorm[i].reshape(A_HEADS, 1, A_DV).astype(F32), lam_init)
        assert (2 * m_qk + m_v) % m_v == 0
        h = _post_call(h1, hf.reshape(t, -1), hb.reshape(t, -1), zm, (2 * m_qk + m_v) // m_v, ya.reshape(t, -1),
                       p[i].reshape(t, -1), row(g_mnorm[i]),
                       wout, row(g_ffn2[i]), win2, wo2, row(g_ple[i]), wpg, wpp, row(g_final),
                       final_norm=(i == depth - 1))
    return h.reshape(bsz, s, d)
```

```python
import functools
import math

import jax
import jax.numpy as jnp
import numpy as np
from jax import lax
from jax.experimental import pallas as pl
from jax.experimental.pallas import tpu as pltpu

F32 = jnp.float32
BF16 = jnp.bfloat16

EPS = 1e-6
LOG2E = 1.4426950408889634

M_HEADS = 4
M_DK = 64
M_DV = 128
A_HEADS = 4
A_DK = 64
A_DV = 128
CONV_K = 3

LANES_V7X = 128
SUBLANES_V7X = 8
BF16_ROWS_PER_VREG_V7X = 16
VMEM_LIMIT_BYTES_V7X = 56 * 1024 * 1024

TOKEN_TILE = 512
FF_CHUNK = 256
MLSTM_CHUNK = 128
MLSTM_CHUNKS_PER_STEP = 8
MLSTM_ROW_BLOCK = 128
ATTN_Q_TILE = 256
ATTN_KEY_BLOCK = 256


def _aligned(x, m):
    return x if isinstance(x, int) else pl.multiple_of(x, m)


def _dot(a, b):
    return jnp.dot(a, b, preferred_element_type=F32)


def _dot_nt(a, b):
    return lax.dot_general(a, b, (((1,), (1,)), ((), ())), preferred_element_type=F32)


def _rmsnorm(x, g):
    ms = jnp.mean(x * x, axis=-1, keepdims=True)
    return x * lax.rsqrt(ms + EPS) * g


def _swiglu(xn, win_ref, wo_ref):
    f = wo_ref.shape[0]
    parts = []
    for c0 in range(0, f, FF_CHUNK):
        g = _dot(xn, win_ref[:, c0:c0 + FF_CHUNK])
        u = _dot(xn, win_ref[:, f + c0:f + c0 + FF_CHUNK])
        parts.append((g * jax.nn.sigmoid(g) * u).astype(BF16))
    return _dot(jnp.concatenate(parts, axis=1), wo_ref[...])


def _resident(shape):
    nd = len(shape)
    return pl.BlockSpec(shape, lambda *_: (0,) * nd, pipeline_mode=pl.Buffered(1))


def _pre_kernel(x_ref, g1_ref, win_ref, wo_ref, gm_ref, wm_ref, wa_ref, wgt_ref, *rest):
    n_cast = (len(rest) - 5) // 2
    cast_in, (h_ref, zm_ref, za_ref, vt_ref, zgt_ref), cast_out = (
        rest[:n_cast], rest[n_cast:n_cast + 5], rest[n_cast + 5:])
    x = x_ref[...]
    xn = _rmsnorm(x, g1_ref[...]).astype(BF16)
    h = x + 0.5 * _swiglu(xn, win_ref, wo_ref)
    h_ref[...] = h
    u = _rmsnorm(h, gm_ref[...]).astype(BF16)
    zm_ref[...] = _dot(u, wm_ref[...]).astype(BF16)
    za = _dot(u, wa_ref[...])
    nqk = za_ref.shape[1]
    za_ref[...] = za[:, :nqk].astype(BF16)
    vt = za[:, nqk:].T.astype(BF16)
    one = jnp.ones((BF16_ROWS_PER_VREG_V7X, vt.shape[1]), BF16)
    vt_ref[...] = jnp.concatenate(
        [blk for c0 in range(0, vt.shape[0], A_DV) for blk in (vt[c0:c0 + A_DV], one)], axis=0)
    zgt_ref[...] = _dot_nt(wgt_ref[...], u)
    for src, dst in zip(cast_in, cast_out):
        dst[...] = src[0].astype(BF16)


def _slab_rows(rows, steps):
    for r in range(BF16_ROWS_PER_VREG_V7X, rows + 1, BF16_ROWS_PER_VREG_V7X):
        if rows % r == 0 and rows // r <= steps:
            return r
    raise ValueError((rows, steps))


def _pre_call(h, g1, win, wo, gm, wm, wa, wgt, layer, later_weights):
    t, d = h.shape
    tm = TOKEN_TILE
    ng = wgt.shape[0]
    n_av = A_HEADS * A_DV
    n_vt = A_HEADS * (A_DV + BF16_ROWS_PER_VREG_V7X)
    steps = t // tm
    row = lambda n: pl.BlockSpec((tm, n), lambda i: (i, 0))
    weights = (g1, win, wo, gm, wm, wa, wgt)
    slabs_in, slabs_out = [], []
    for w in later_weights:
        r = _slab_rows(w.shape[1], steps)
        last = w.shape[1] // r - 1
        slabs_in.append(pl.BlockSpec((1, r, w.shape[2]), lambda i, last=last: (layer, jnp.minimum(i, last), 0)))
        slabs_out.append(pl.BlockSpec((r, w.shape[2]), lambda i, last=last: (jnp.minimum(i, last), 0)))
    outs = pl.pallas_call(
        _pre_kernel,
        grid=(steps,),
        in_specs=[row(d)] + [_resident(w.shape) for w in weights] + slabs_in,
        out_specs=[row(d), row(wm.shape[1]), row(wa.shape[1] - n_av),
                   pl.BlockSpec((n_vt, tm), lambda i: (0, i)), pl.BlockSpec((ng, tm), lambda i: (0, i))]
                  + slabs_out,
        out_shape=[jax.ShapeDtypeStruct((t, d), F32),
                   jax.ShapeDtypeStruct((t, wm.shape[1]), BF16),
                   jax.ShapeDtypeStruct((t, wa.shape[1] - n_av), BF16),
                   jax.ShapeDtypeStruct((n_vt, t), BF16),
                   jax.ShapeDtypeStruct((ng, t), F32)]
                  + [jax.ShapeDtypeStruct(w.shape[1:], BF16) for w in later_weights],
        compiler_params=pltpu.CompilerParams(
            dimension_semantics=("arbitrary",), vmem_limit_bytes=VMEM_LIMIT_BYTES_V7X),
        name="pre_ffn_inproj",
    )(h, *weights, *later_weights)
    return outs[:5], outs[5:]


def _post_kernel(h_ref, hf_ref, hb_ref, og_ref, ya_ref, p_ref, gn_ref, wout_ref, g2_ref, win_ref, wo_ref,
                 gp_ref, wpg_ref, wpp_ref, gf_ref, o_ref, *, final_norm):
    dm = hf_ref.shape[1]
    h = h_ref[...] + _dot(ya_ref[...], wout_ref[dm:, :])
    heads = []
    for c0 in range(0, dm, M_DV):
        hh = hf_ref[:, c0:c0 + M_DV] + hb_ref[:, c0:c0 + M_DV]
        hn = hh * lax.rsqrt(jnp.mean(hh * hh, axis=-1, keepdims=True) + EPS) * gn_ref[:, c0:c0 + M_DV]
        heads.append((jax.nn.sigmoid(og_ref[:, c0:c0 + M_DV].astype(F32)) * hn).astype(BF16))
    h = h + _dot(jnp.concatenate(heads, axis=1), wout_ref[0:dm, :])
    xn = _rmsnorm(h, g2_ref[...]).astype(BF16)
    h = h + 0.5 * _swiglu(xn, win_ref, wo_ref)
    gate = jax.nn.sigmoid(_dot(_rmsnorm(h, gp_ref[...]).astype(BF16), wpg_ref[...]))
    h = h + gate * _dot(p_ref[...].astype(BF16), wpp_ref[...])
    if final_norm:
        h = _rmsnorm(h, gf_ref[...])
    o_ref[...] = h


def _post_call(h1, hf, hb, zm, og_block, ya, p, gn, wout, g2, win, wo, gp, wpg, wpp, gf, final_norm):
    t, d = h1.shape
    tm = TOKEN_TILE
    row = lambda n: pl.BlockSpec((tm, n), lambda i: (i, 0))
    weights = (gn, wout, g2, win, wo, gp, wpg, wpp, gf)
    return pl.pallas_call(
        functools.partial(_post_kernel, final_norm=final_norm),
        grid=(t // tm,),
        in_specs=[row(d), row(hf.shape[1]), row(hb.shape[1]),
                  pl.BlockSpec((tm, hf.shape[1]), lambda i: (i, og_block)), row(ya.shape[1]), row(p.shape[1])]
                 + [_resident(w.shape) for w in weights],
        out_specs=row(d),
        out_shape=jax.ShapeDtypeStruct((t, d), F32),
        compiler_params=pltpu.CompilerParams(
            dimension_semantics=("arbitrary",), vmem_limit_bytes=VMEM_LIMIT_BYTES_V7X),
        name="post_outproj_ffn_ple",
    )(h1, hf, hb, zm, ya, p, *weights)


def _attn_kernel(lam_ref, slope_ref, g_ref, q_ref, k_ref, vt_ref, qn_ref, kn_ref, o_ref,
                 tab_s, st0_s, st1_s, m0_s, *, lam_init, tq, kblk):
    s = k_ref.shape[1]
    nq = s // tq
    nkb = s // kblk
    first_row = pl.program_id(1) == 0

    @pl.when(first_row)
    def _():
        r = lax.broadcasted_iota(jnp.int32, tab_s.shape, 1)
        c = lax.broadcasted_iota(jnp.int32, tab_s.shape, 0)
        tab_s[...] = jnp.abs(r - c + (s - tq)).astype(F32) * slope_ref[0, :, 0:1]

    lam_v = lam_ref[...]
    lam = (jnp.exp(jnp.sum(lam_v[0:1] * lam_v[1:2], axis=1, keepdims=True))
           - jnp.exp(jnp.sum(lam_v[2:3] * lam_v[3:4], axis=1, keepdims=True)) + lam_init)

    gain = g_ref[0] * (1.0 - lam_init)
    lane = lax.broadcasted_iota(jnp.int32, (tq, 2 * A_DK), 1)
    neg = jnp.full((1, 2 * tq), -jnp.inf, F32)
    acc0 = jnp.zeros((vt_ref.shape[0], 2 * tq), F32)

    def stage(tile, q, keys_ref, dst, m_prev, src):
        zero = jnp.zeros_like(q)
        qm = jnp.concatenate([jnp.where(lane < A_DK, q, zero), jnp.where(lane >= A_DK, q, zero)], axis=0)
        mx, acc = neg, acc0
        for kb in range(nkb):
            rows = slice(kb * kblk, (kb + 1) * kblk)
            bias = tab_s[pl.ds(_aligned((nq - 1 - tile) * tq + kb * kblk, kblk), kblk), :]
            sc = _dot_nt(keys_ref[0, rows, :], qm) - jnp.concatenate([bias, bias], axis=1)
            dst[rows, :] = sc
            mx = jnp.maximum(mx, jnp.max(sc, axis=0, keepdims=True))
            if src is not None:
                e = jnp.exp2(src[rows, :] - m_prev).astype(BF16)
                acc = acc + _dot(vt_ref[:, rows], e)
        return mx, acc

    def finish(u, acc):
        r = 1.0 / acc[A_DV:A_DV + 1, :]
        o = acc[:A_DV, :tq] * r[:, :tq] - lam * (acc[:A_DV, tq:] * r[:, tq:])
        on = o * lax.rsqrt(jnp.mean(o * o, axis=0, keepdims=True) + EPS)
        o_ref[0, pl.ds(_aligned(u * tq, tq), tq), :] = (on.T * gain).astype(o_ref.dtype)

    assert nq % 2 == 0 and nq >= 4

    def q_tile(u):
        return q_ref[0, pl.ds(_aligned(u * tq, tq), tq), :]

    @pl.when(first_row)
    def _():
        m0_s[...] = stage(0, q_tile(0), k_ref, st0_s, None, None)[0]

    def pair(j, carry, last):
        m_even, acc_odd = carry
        if acc_odd is not None:
            finish(2 * j - 1, acc_odd)
        m_odd, acc_even = stage(2 * j + 1, q_tile(2 * j + 1), k_ref, st1_s, m_even, st0_s)
        finish(2 * j, acc_even)
        if last:
            return stage(0, qn_ref[0], kn_ref, st0_s, m_odd, st1_s)
        return stage(2 * j + 2, q_tile(2 * j + 2), k_ref, st0_s, m_odd, st1_s)

    carry = pair(0, (m0_s[...], None), False)
    carry = lax.fori_loop(1, nq // 2 - 1, lambda j, c: pair(j, c, False), carry)
    m_next, acc_last = pair(nq // 2 - 1, carry, True)
    finish(nq - 1, acc_last)
    m0_s[...] = m_next


def _attn_call(za, vt, lam_vec, slopes, g_anorm, lam_init):
    b, s, _ = za.shape
    tq = ATTN_Q_TILE
    blk = 2 * A_DK
    assert blk == A_DV == LANES_V7X
    head_block = lambda off: pl.BlockSpec((1, s, blk), lambda h, i: (i, 0, off + h))
    nxt = lambda i: jnp.minimum(i + 1, b - 1)
    return pl.pallas_call(
        functools.partial(_attn_kernel, lam_init=lam_init, tq=tq, kblk=ATTN_KEY_BLOCK),
        grid=(A_HEADS, b),
        in_specs=[pl.BlockSpec(lam_vec.shape, lambda h, i: (0, 0)),
                  pl.BlockSpec((1, 1, LANES_V7X), lambda h, i: (h, 0, 0)),
                  pl.BlockSpec((1, 1, A_DV), lambda h, i: (h, 0, 0)),
                  head_block(0), head_block(A_HEADS),
                  pl.BlockSpec((A_DV + BF16_ROWS_PER_VREG_V7X, s), lambda h, i: (h, i)),
                  pl.BlockSpec((1, tq, blk), lambda h, i: (nxt(i), 0, h)),
                  pl.BlockSpec((1, s, blk), lambda h, i: (nxt(i), 0, A_HEADS + h))],
        out_specs=pl.BlockSpec((1, s, A_DV), lambda h, i: (i, 0, h)),
        out_shape=jax.ShapeDtypeStruct((b, s, A_HEADS * A_DV), BF16),
        scratch_shapes=[pltpu.VMEM((2 * s - tq, tq), F32),
                        pltpu.VMEM((s, 2 * tq), F32),
                        pltpu.VMEM((s, 2 * tq), F32),
                        pltpu.VMEM((1, 2 * tq), F32)],
        compiler_params=pltpu.CompilerParams(
            dimension_semantics=("arbitrary", "arbitrary"), vmem_limit_bytes=VMEM_LIMIT_BYTES_V7X),
        name="diff_attention",
    )(lam_vec, slopes, g_anorm, za, za, vt, za, za)


def _log_sigmoid(x):
    return jnp.minimum(x, 0.0) - jnp.log(1.0 + jnp.exp(-jnp.abs(x)))


def _chunk_scan(x, op, fill, pos_in_chunk, chunk, reverse):
    n = x.shape[1]
    k = 1
    while k < chunk:
        if reverse:
            shifted = pltpu.roll(x, n - k, 1)
            valid = pos_in_chunk < chunk - k
        else:
            shifted = pltpu.roll(x, k, 1)
            valid = pos_in_chunk >= k
        x = op(x, jnp.where(valid, shifted, fill))
        k *= 2
    return x


def _lane_replicated(row):
    return jnp.broadcast_to(row, (LANES_V7X, row.shape[1])).T


def _mlstm_kernel(zm_ref, zgt_ref, convw_ref, bg_ref, hf_ref, hb_ref,
                  qx_s, kt_s, ktb_s, brow_s, cmrow_s, crow_s, c_s, m_s, *, chunk, rblk):
    s = zm_ref.shape[1]
    nc = s // chunk
    nseq = 2 * M_HEADS
    dqk = M_HEADS * M_DK

    cw = convw_ref[...]
    lane_q = lax.broadcasted_iota(jnp.int32, (rblk, 2 * M_DK), 1)
    win = 2 * rblk
    sub = SUBLANES_V7X
    rho = lax.broadcasted_iota(jnp.int32, (CONV_K * rblk, win), 0)
    u_i = lax.broadcasted_iota(jnp.int32, (CONV_K * rblk, win), 1)
    src_time = (rho // (CONV_K * sub)) * sub + rho % sub + (rho % (CONV_K * sub)) // sub - 1
    shifts = {}
    for j in range(s // rblk):
        r0 = j * rblk
        lo = min(max(r0 - rblk // 2, 0), s - win)
        off = r0 - lo
        if off not in shifts:
            shifts[off] = jnp.where(u_i == src_time + off, 1.0, 0.0).astype(BF16)
        ys = []
        for c0 in range(0, 2 * dqk, dqk):
            cols = slice(c0, c0 + dqk)
            a = _dot(shifts[off], zm_ref[0, lo:lo + win, cols])
            ys.append(jnp.concatenate(
                [functools.reduce(jnp.add, [a[g + k * sub:g + (k + 1) * sub] * cw[k:k + 1, cols]
                                            for k in range(CONV_K)])
                 for g in range(0, CONV_K * rblk, CONV_K * sub)], axis=0))
        y = jnp.concatenate(ys, axis=1)
        y = y * jax.nn.sigmoid(y)
        q = y[:, :dqk].astype(BF16)
        kk = y[:, dqk:] * (M_DK ** -0.5)
        kt = kk.T
        kt_s[:, r0:r0 + rblk] = kt
        ktb_s[:, r0:r0 + rblk] = kt.astype(BF16)
        for h in range(M_HEADS):
            pair = q[:, (h // 2) * 2 * M_DK:(h // 2 + 1) * 2 * M_DK]
            keep = (lane_q >= M_DK) if (h % 2) else (lane_q < M_DK)
            qx_s[r0:r0 + rblk, h * 2 * M_DK:(h + 1) * 2 * M_DK] = jnp.where(keep, pair, jnp.zeros_like(pair))

    g = zgt_ref[...] + bg_ref[...]
    li = g[0:nseq] * LOG2E
    lf = _log_sigmoid(g[nseq:2 * nseq]) * LOG2E
    pos = lax.broadcasted_iota(jnp.int32, li.shape, 1) % chunk
    is_fw = lax.broadcasted_iota(jnp.int32, li.shape, 0) < M_HEADS
    b = jnp.where(is_fw, _chunk_scan(lf, jnp.add, 0.0, pos, chunk, False),
                  _chunk_scan(lf, jnp.add, 0.0, pos, chunk, True))
    c = li - b
    cmax = jnp.where(is_fw, _chunk_scan(c, jnp.maximum, -jnp.inf, pos, chunk, False),
                     _chunk_scan(c, jnp.maximum, -jnp.inf, pos, chunk, True))
    for j in range(nc):
        cols = slice(j * chunk, (j + 1) * chunk)
        brow_s[j] = b[:, cols]
        cmrow_s[j] = cmax[:, cols]
        crow_s[j] = c[:, cols]

    c_s[...] = jnp.zeros_like(c_s)
    m_s[...] = jnp.zeros_like(m_s)

    rr = lax.broadcasted_iota(jnp.int32, (chunk, chunk), 0)
    cc = lax.broadcasted_iota(jnp.int32, (chunk, chunk), 1)
    causal = (cc <= rr, cc >= rr)
    ones = jnp.ones((chunk, M_DV), BF16)

    def step(j, carry):
        for d in range(2):
            cj = j if d == 0 else nc - 1 - j
            rows = pl.ds(pl.multiple_of(cj * chunk, chunk), chunk)
            last = chunk - 1 if d == 0 else 0
            h_out = hf_ref if d == 0 else hb_ref
            for pr in range(M_HEADS // 2):
                state = c_s[d, pr]
                state_bf = state.astype(BF16)
                ktp = ktb_s[pr * 2 * M_DK:(pr + 1) * 2 * M_DK, rows]
                for half in range(2):
                    h = 2 * pr + half
                    i = d * M_HEADS + h
                    blk = slice(h * M_DV, (h + 1) * M_DV)
                    qc = qx_s[rows, blk]
                    v_aug = jnp.concatenate(
                        [zm_ref[0, rows, 2 * dqk + h * M_DV:2 * dqk + (h + 1) * M_DV], ones], axis=1)
                    crow = crow_s[cj, i:i + 1, :]
                    m_old = m_s[i:i + 1, :]
                    b_rep = _lane_replicated(brow_s[cj, i:i + 1, :])
                    m_rep = jnp.maximum(m_old, _lane_replicated(cmrow_s[cj, i:i + 1, :]))
                    dt = jnp.where(causal[d], jnp.exp2(crow - m_rep), 0.0)
                    sm = (_dot(qc, ktp) * dt).astype(BF16)
                    qw = (qc * jnp.exp2(m_old - m_rep)).astype(BF16)
                    tot = _dot(jnp.concatenate([qw, sm], axis=1),
                               jnp.concatenate([state_bf, v_aug], axis=0))
                    floor = jnp.exp2(-(b_rep + m_rep))
                    h_out[0, rows, blk] = tot[:, :M_DV] / jnp.maximum(jnp.abs(tot[:, M_DV:]), floor)
                    m_last = m_rep[last:last + 1, :]
                    wk = jnp.exp2(crow - m_last[:, 0:1])
                    kth = kt_s[h * M_DK:(h + 1) * M_DK, rows]
                    dc = _dot((kth * wk).astype(BF16), v_aug)
                    sub = slice(half * M_DK, (half + 1) * M_DK)
                    c_s[d, pr, sub, :] = jnp.exp2(m_old - m_last)[:, 0:1] * state[sub] + dc
                    m_s[i:i + 1, :] = b_rep[last:last + 1, :] + m_last
        return carry

    def steps(jj, carry):
        for k in range(MLSTM_CHUNKS_PER_STEP):
            step(jj * MLSTM_CHUNKS_PER_STEP + k, carry)
        return carry

    lax.fori_loop(0, nc // MLSTM_CHUNKS_PER_STEP, steps, 0)


def _mlstm_call(zm, zgt, conv_w, bg):
    b, s, wz = zm.shape
    chunk = MLSTM_CHUNK
    nc = s // chunk
    dv = M_HEADS * M_DV
    dqk = M_HEADS * M_DK
    nseq = 2 * M_HEADS
    full = lambda a: pl.BlockSpec(a.shape, lambda i: (0,) * a.ndim)
    per_b = lambda n: pl.BlockSpec((1, s, n), lambda i: (i, 0, 0))
    gate_rows = lambda: pltpu.VMEM((nc, nseq, chunk), F32)
    return pl.pallas_call(
        functools.partial(_mlstm_kernel, chunk=chunk, rblk=MLSTM_ROW_BLOCK),
        grid=(b,),
        in_specs=[per_b(wz), pl.BlockSpec((zgt.shape[0], s), lambda i: (0, i)), full(conv_w), full(bg)],
        out_specs=[per_b(dv), per_b(dv)],
        out_shape=[jax.ShapeDtypeStruct((b, s, dv), F32), jax.ShapeDtypeStruct((b, s, dv), F32)],
        scratch_shapes=[
            pltpu.VMEM((s, M_HEADS * 2 * M_DK), BF16),
            pltpu.VMEM((dqk, s), F32),
            pltpu.VMEM((dqk, s), BF16),
            gate_rows(),
            gate_rows(),
            gate_rows(),
            pltpu.VMEM((2, M_HEADS // 2, 2 * M_DK, 2 * M_DV), F32),
            pltpu.VMEM((nseq, LANES_V7X), F32),
        ],
        compiler_params=pltpu.CompilerParams(
            dimension_semantics=("arbitrary",), vmem_limit_bytes=VMEM_LIMIT_BYTES_V7X),
        name="mlstm_bidir",
    )(zm, zgt, conv_w, bg)


def _gate_rows(a16):
    hh = M_HEADS
    return jnp.concatenate([a16[..., 0:hh], a16[..., 2 * hh:3 * hh], a16[..., hh:2 * hh], a16[..., 3 * hh:]],
                           axis=-1)


def kernel(x, p, g_ffn1, w_ffn1_in, w_ffn1_out, g_mix, w_in, b_mgate, conv_w, g_mnorm, lam_q1, lam_k1, lam_q2, lam_k2, g_anorm, w_out, g_ffn2, w_ffn2_in, w_ffn2_out, g_ple, w_ple_gate, w_ple_proj, g_final):
    bsz, s, d = x.shape
    depth = w_in.shape[0]
    t = bsz * s
    m_qk, m_v, m_g = M_HEADS * M_DK, M_HEADS * M_DV, 4 * M_HEADS
    a_qk = A_HEADS * 2 * A_DK
    m_end = 2 * m_qk + 2 * m_v
    assert w_in.shape[2] == m_end + m_g + 2 * a_qk + A_HEADS * A_DV and conv_w.shape[1:] == (CONV_K, 2 * m_qk)
    assert t % TOKEN_TILE == 0 and w_ffn1_out.shape[1] % FF_CHUNK == 0 and w_ffn2_out.shape[1] % FF_CHUNK == 0
    assert s % (MLSTM_CHUNK * MLSTM_CHUNKS_PER_STEP) == 0 and s % MLSTM_ROW_BLOCK == 0
    assert s % ATTN_Q_TILE == 0 and ATTN_Q_TILE % ATTN_KEY_BLOCK == 0
    slopes = np.array([2.0 ** (-8.0 * (h + 1) / A_HEADS) for h in range(A_HEADS)], dtype=np.float32)
    slopes = jnp.broadcast_to(jnp.asarray(slopes * np.float32(LOG2E))[:, None, None],
                              (A_HEADS, 1, LANES_V7X))
    row = lambda v: v.reshape(1, -1).astype(F32)

    h = x.reshape(t, d)
    for i in range(depth):
        lam_init = 0.8 - 0.6 * math.exp(-0.3 * i)
        wi = w_in[i]
        wm = wi[:, :m_end].astype(BF16)
        wa = wi[:, m_end + m_g:]
        wa = jnp.concatenate([wa[:, :a_qk] * (A_DK ** -0.5 * LOG2E), wa[:, a_qk:]], axis=1).astype(BF16)
        wgt = _gate_rows(wi[:, m_end:m_end + m_g]).T.astype(BF16)
        bg = _gate_rows(b_mgate[i].astype(F32)).reshape(m_g, 1)

        (h1, zm, za, vt, zgt), (win2, wo2, wout, wpg, wpp) = _pre_call(
            h, row(g_ffn1[i]), w_ffn1_in[i].astype(BF16), w_ffn1_out[i].astype(BF16), row(g_mix[i]), wm, wa, wgt,
            i, (w_ffn2_in, w_ffn2_out, w_out, w_ple_gate, w_ple_proj))
        hf, hb = _mlstm_call(zm.reshape(bsz, s, -1), zgt, conv_w[i].astype(F32), bg)
        lam_vec = jnp.stack([lam_q1[i], lam_k1[i], lam_q2[i], lam_k2[i]]).astype(F32)
        ya = _attn_call(za.reshape(bsz, s, -1), vt, lam_vec, slopes,
                        g_anorm[i].reshape(A_HEADS, 1, A_DV).astype(F32), lam_init)
        assert (2 * m_qk + m_v) % m_v == 0
        h = _post_call(h1, hf.reshape(t, -1), hb.reshape(t, -1), zm, (2 * m_qk + m_v) // m_v, ya.reshape(t, -1),
                       p[i].reshape(t, -1), row(g_mnorm[i]),
                       wout, row(g_ffn2[i]), win2, wo2, row(g_ple[i]), wpg, wpp, row(g_final),
                       final_norm=(i == depth - 1))
    return h.reshape(bsz, s, d)
```

```python
import functools
import math

import jax
import jax.numpy as jnp
import numpy as np
from jax import lax
from jax.experimental import pallas as pl
from jax.experimental.pallas import tpu as pltpu

F32 = jnp.float32
BF16 = jnp.bfloat16

EPS = 1e-6
LOG2E = 1.4426950408889634

M_HEADS = 4
M_DK = 64
M_DV = 128
A_HEADS = 4
A_DK = 64
A_DV = 128
CONV_K = 3

LANES_V7X = 128
SUBLANES_V7X = 8
BF16_ROWS_PER_VREG_V7X = 16
VMEM_LIMIT_BYTES_V7X = 56 * 1024 * 1024

TOKEN_TILE = 512
FF_CHUNK = 256
MLSTM_CHUNK = 128
MLSTM_CHUNKS_PER_STEP = 8
MLSTM_ROW_BLOCK = 128
ATTN_Q_TILE = 256
ATTN_KEY_BLOCK = 256


def _aligned(x, m):
    return x if isinstance(x, int) else pl.multiple_of(x, m)


def _dot(a, b):
    return jnp.dot(a, b, preferred_element_type=F32)


def _dot_nt(a, b):
    return lax.dot_general(a, b, (((1,), (1,)), ((), ())), preferred_element_type=F32)


def _rmsnorm(x, g):
    ms = jnp.mean(x * x, axis=-1, keepdims=True)
    return x * lax.rsqrt(ms + EPS) * g


def _swiglu(xn, win_ref, wo_ref):
    f = wo_ref.shape[0]
    parts = []
    for c0 in range(0, f, FF_CHUNK):
        g = _dot(xn, win_ref[:, c0:c0 + FF_CHUNK])
        u = _dot(xn, win_ref[:, f + c0:f + c0 + FF_CHUNK])
        parts.append((g * jax.nn.sigmoid(g) * u).astype(BF16))
    return _dot(jnp.concatenate(parts, axis=1), wo_ref[...])


def _resident(shape):
    nd = len(shape)
    return pl.BlockSpec(shape, lambda *_: (0,) * nd, pipeline_mode=pl.Buffered(1))


def _pre_kernel(x_ref, g1_ref, win_ref, wo_ref, gm_ref, wm_ref, wa_ref, wgt_ref, *rest):
    n_cast = (len(rest) - 5) // 2
    cast_in, (h_ref, zm_ref, za_ref, vt_ref, zgt_ref), cast_out = (
        rest[:n_cast], rest[n_cast:n_cast + 5], rest[n_cast + 5:])
    x = x_ref[...]
    xn = _rmsnorm(x, g1_ref[...]).astype(BF16)
    h = x + 0.5 * _swiglu(xn, win_ref, wo_ref)
    h_ref[...] = h
    u = _rmsnorm(h, gm_ref[...]).astype(BF16)
    zm_ref[...] = _dot(u, wm_ref[...]).astype(BF16)
    za = _dot(u, wa_ref[...])
    nqk = za_ref.shape[1]
    za_ref[...] = za[:, :nqk].astype(BF16)
    vt = za[:, nqk:].T.astype(BF16)
    one = jnp.ones((BF16_ROWS_PER_VREG_V7X, vt.shape[1]), BF16)
    vt_ref[...] = jnp.concatenate(
        [blk for c0 in range(0, vt.shape[0], A_DV) for blk in (vt[c0:c0 + A_DV], one)], axis=0)
    zgt_ref[...] = _dot_nt(wgt_ref[...], u)
    for src, dst in zip(cast_in, cast_out):
        dst[...] = src[0].astype(BF16)


def _slab_rows(rows, steps):
    for r in range(BF16_ROWS_PER_VREG_V7X, rows + 1, BF16_ROWS_PER_VREG_V7X):
        if rows % r == 0 and rows // r <= steps:
            return r
    raise ValueError((rows, steps))


def _pre_call(h, g1, win, wo, gm, wm, wa, wgt, layer, later_weights):
    t, d = h.shape
    tm = TOKEN_TILE
    ng = wgt.shape[0]
    n_av = A_HEADS * A_DV
    n_vt = A_HEADS * (A_DV + BF16_ROWS_PER_VREG_V7X)
    steps = t // tm
    row = lambda n: pl.BlockSpec((tm, n), lambda i: (i, 0))
    weights = (g1, win, wo, gm, wm, wa, wgt)
    slabs_in, slabs_out = [], []
    for w in later_weights:
        r = _slab_rows(w.shape[1], steps)
        last = w.shape[1] // r - 1
        slabs_in.append(pl.BlockSpec((1, r, w.shape[2]), lambda i, last=last: (layer, jnp.minimum(i, last), 0)))
        slabs_out.append(pl.BlockSpec((r, w.shape[2]), lambda i, last=last: (jnp.minimum(i, last), 0)))
    outs = pl.pallas_call(
        _pre_kernel,
        grid=(steps,),
        in_specs=[row(d)] + [_resident(w.shape) for w in weights] + slabs_in,
        out_specs=[row(d), row(wm.shape[1]), row(wa.shape[1] - n_av),
                   pl.BlockSpec((n_vt, tm), lambda i: (0, i)), pl.BlockSpec((ng, tm), lambda i: (0, i))]
                  + slabs_out,
        out_shape=[jax.ShapeDtypeStruct((t, d), F32),
                   jax.ShapeDtypeStruct((t, wm.shape[1]), BF16),
                   jax.ShapeDtypeStruct((t, wa.shape[1] - n_av), BF16),
                   jax.ShapeDtypeStruct((n_vt, t), BF16),
                   jax.ShapeDtypeStruct((ng, t), F32)]
                  + [jax.ShapeDtypeStruct(w.shape[1:], BF16) for w in later_weights],
        compiler_params=pltpu.CompilerParams(
            dimension_semantics=("arbitrary",), vmem_limit_bytes=VMEM_LIMIT_BYTES_V7X),
        name="pre_ffn_inproj",
    )(h, *weights, *later_weights)
    return outs[:5], outs[5:]


def _post_kernel(h_ref, hf_ref, hb_ref, og_ref, ya_ref, p_ref, gn_ref, wout_ref, g2_ref, win_ref, wo_ref,
                 gp_ref, wpg_ref, wpp_ref, gf_ref, o_ref, *, final_norm):
    dm = hf_ref.shape[1]
    h = h_ref[...] + _dot(ya_ref[...], wout_ref[dm:, :])
    heads = []
    for c0 in range(0, dm, M_DV):
        hh = hf_ref[:, c0:c0 + M_DV] + hb_ref[:, c0:c0 + M_DV]
        hn = hh * lax.rsqrt(jnp.mean(hh * hh, axis=-1, keepdims=True) + EPS) * gn_ref[:, c0:c0 + M_DV]
        heads.append((jax.nn.sigmoid(og_ref[:, c0:c0 + M_DV].astype(F32)) * hn).astype(BF16))
    h = h + _dot(jnp.concatenate(heads, axis=1), wout_ref[0:dm, :])
    xn = _rmsnorm(h, g2_ref[...]).astype(BF16)
    h = h + 0.5 * _swiglu(xn, win_ref, wo_ref)
    gate = jax.nn.sigmoid(_dot(_rmsnorm(h, gp_ref[...]).astype(BF16), wpg_ref[...]))
    h = h + gate * _dot(p_ref[...].astype(BF16), wpp_ref[...])
    if final_norm:
        h = _rmsnorm(h, gf_ref[...])
    o_ref[...] = h


def _post_call(h1, hf, hb, zm, og_block, ya, p, gn, wout, g2, win, wo, gp, wpg, wpp, gf, final_norm):
    t, d = h1.shape
    tm = TOKEN_TILE
    row = lambda n: pl.BlockSpec((tm, n), lambda i: (i, 0))
    weights = (gn, wout, g2, win, wo, gp, wpg, wpp, gf)
    return pl.pallas_call(
        functools.partial(_post_kernel, final_norm=final_norm),
        grid=(t // tm,),
        in_specs=[row(d), row(hf.shape[1]), row(hb.shape[1]),
                  pl.BlockSpec((tm, hf.shape[1]), lambda i: (i, og_block)), row(ya.shape[1]), row(p.shape[1])]
                 + [_resident(w.shape) for w in weights],
        out_specs=row(d),
        out_shape=jax.ShapeDtypeStruct((t, d), F32),
        compiler_params=pltpu.CompilerParams(
            dimension_semantics=("arbitrary",), vmem_limit_bytes=VMEM_LIMIT_BYTES_V7X),
        name="post_outproj_ffn_ple",
    )(h1, hf, hb, zm, ya, p, *weights)


def _attn_kernel(lam_ref, slope_ref, g_ref, q_ref, k_ref, vt_ref, qn_ref, kn_ref, o_ref,
                 tab_s, st0_s, st1_s, m0_s, *, lam_init, tq, kblk):
    s = k_ref.shape[1]
    nq = s // tq
    nkb = s // kblk
    first_row = pl.program_id(1) == 0

    @pl.when(first_row)
    def _():
        r = lax.broadcasted_iota(jnp.int32, tab_s.shape, 1)
        c = lax.broadcasted_iota(jnp.int32, tab_s.shape, 0)
        tab_s[...] = jnp.abs(r - c + (s - tq)).astype(F32) * slope_ref[0, :, 0:1]

    lam_v = lam_ref[...]
    lam = (jnp.exp(jnp.sum(lam_v[0:1] * lam_v[1:2], axis=1, keepdims=True))
           - jnp.exp(jnp.sum(lam_v[2:3] * lam_v[3:4], axis=1, keepdims=True)) + lam_init)

    gain = g_ref[0] * (1.0 - lam_init)
    lane = lax.broadcasted_iota(jnp.int32, (tq, 2 * A_DK), 1)
    neg = jnp.full((1, 2 * tq), -jnp.inf, F32)
    acc0 = jnp.zeros((vt_ref.shape[0], 2 * tq), F32)

    def stage(tile, q, keys_ref, dst, m_prev, src):
        zero = jnp.zeros_like(q)
        qm = jnp.concatenate([jnp.where(lane < A_DK, q, zero), jnp.where(lane >= A_DK, q, zero)], axis=0)
        mx, acc = neg, acc0
        for kb in range(nkb):
            rows = slice(kb * kblk, (kb + 1) * kblk)
            bias = tab_s[pl.ds(_aligned((nq - 1 - tile) * tq + kb * kblk, kblk), kblk), :]
            sc = _dot_nt(keys_ref[0, rows, :], qm) - jnp.concatenate([bias, bias], axis=1)
            dst[rows, :] = sc
            mx = jnp.maximum(mx, jnp.max(sc, axis=0, keepdims=True))
            if src is not None:
                e = jnp.exp2(src[rows, :] - m_prev).astype(BF16)
                acc = acc + _dot(vt_ref[:, rows], e)
        return mx, acc

    def finish(u, acc):
        r = 1.0 / acc[A_DV:A_DV + 1, :]
        o = acc[:A_DV, :tq] * r[:, :tq] - lam * (acc[:A_DV, tq:] * r[:, tq:])
        on = o * lax.rsqrt(jnp.mean(o * o, axis=0, keepdims=True) + EPS)
        o_ref[0, pl.ds(_aligned(u * tq, tq), tq), :] = (on.T * gain).astype(o_ref.dtype)

    assert nq % 2 == 0 and nq >= 4

    def q_tile(u):
        return q_ref[0, pl.ds(_aligned(u * tq, tq), tq), :]

    @pl.when(first_row)
    def _():
        m0_s[...] = stage(0, q_tile(0), k_ref, st0_s, None, None)[0]

    def pair(j, carry, last):
        m_even, acc_odd = carry
        if acc_odd is not None:
            finish(2 * j - 1, acc_odd)
        m_odd, acc_even = stage(2 * j + 1, q_tile(2 * j + 1), k_ref, st1_s, m_even, st0_s)
        finish(2 * j, acc_even)
        if last:
            return stage(0, qn_ref[0], kn_ref, st0_s, m_odd, st1_s)
        return stage(2 * j + 2, q_tile(2 * j + 2), k_ref, st0_s, m_odd, st1_s)

    carry = pair(0, (m0_s[...], None), False)
    carry = lax.fori_loop(1, nq // 2 - 1, lambda j, c: pair(j, c, False), carry)
    m_next, acc_last = pair(nq // 2 - 1, carry, True)
    finish(nq - 1, acc_last)
    m0_s[...] = m_next


def _attn_call(za, vt, lam_vec, slopes, g_anorm, lam_init):
    b, s, _ = za.shape
    tq = ATTN_Q_TILE
    blk = 2 * A_DK
    assert blk == A_DV == LANES_V7X
    head_block = lambda off: pl.BlockSpec((1, s, blk), lambda h, i: (i, 0, off + h))
    nxt = lambda i: jnp.minimum(i + 1, b - 1)
    return pl.pallas_call(
        functools.partial(_attn_kernel, lam_init=lam_init, tq=tq, kblk=ATTN_KEY_BLOCK),
        grid=(A_HEADS, b),
        in_specs=[pl.BlockSpec(lam_vec.shape, lambda h, i: (0, 0)),
                  pl.BlockSpec((1, 1, LANES_V7X), lambda h, i: (h, 0, 0)),
                  pl.BlockSpec((1, 1, A_DV), lambda h, i: (h, 0, 0)),
                  head_block(0), head_block(A_HEADS),
                  pl.BlockSpec((A_DV + BF16_ROWS_PER_VREG_V7X, s), lambda h, i: (h, i)),
                  pl.BlockSpec((1, tq, blk), lambda h, i: (nxt(i), 0, h)),
                  pl.BlockSpec((1, s, blk), lambda h, i: (nxt(i), 0, A_HEADS + h))],
        out_specs=pl.BlockSpec((1, s, A_DV), lambda h, i: (i, 0, h)),
        out_shape=jax.ShapeDtypeStruct((b, s, A_HEADS * A_DV), BF16),
        scratch_shapes=[pltpu.VMEM((2 * s - tq, tq), F32),
                        pltpu.VMEM((s, 2 * tq), F32),
                        pltpu.VMEM((s, 2 * tq), F32),
                        pltpu.VMEM((1, 2 * tq), F32)],
        compiler_params=pltpu.CompilerParams(
            dimension_semantics=("arbitrary", "arbitrary"), vmem_limit_bytes=VMEM_LIMIT_BYTES_V7X),
        name="diff_attention",
    )(lam_vec, slopes, g_anorm, za, za, vt, za, za)


def _log_sigmoid(x):
    return jnp.minimum(x, 0.0) - jnp.log(1.0 + jnp.exp(-jnp.abs(x)))


def _chunk_scan(x, op, fill, pos_in_chunk, chunk, reverse):
    n = x.shape[1]
    k = 1
    while k < chunk:
        if reverse:
            shifted = pltpu.roll(x, n - k, 1)
            valid = pos_in_chunk < chunk - k
        else:
            shifted = pltpu.roll(x, k, 1)
            valid = pos_in_chunk >= k
        x = op(x, jnp.where(valid, shifted, fill))
        k *= 2
    return x


def _lane_replicated(row):
    return jnp.broadcast_to(row, (LANES_V7X, row.shape[1])).T


def _mlstm_kernel(zm_ref, zgt_ref, convw_ref, bg_ref, hf_ref, hb_ref,
                  qx_s, kt_s, ktb_s, brow_s, cmrow_s, crow_s, c_s, m_s, *, chunk, rblk):
    s = zm_ref.shape[1]
    nc = s // chunk
    nseq = 2 * M_HEADS
    dqk = M_HEADS * M_DK

    cw = convw_ref[...]
    lane_q = lax.broadcasted_iota(jnp.int32, (rblk, 2 * M_DK), 1)
    win = 2 * rblk
    sub = SUBLANES_V7X
    rho = lax.broadcasted_iota(jnp.int32, (CONV_K * rblk, win), 0)
    u_i = lax.broadcasted_iota(jnp.int32, (CONV_K * rblk, win), 1)
    src_time = (rho // (CONV_K * sub)) * sub + rho % sub + (rho % (CONV_K * sub)) // sub - CONV_K // 2
    shifts = {}
    for j in range(s // rblk):
        r0 = j * rblk
        lo = min(max(r0 - rblk // 2, 0), s - win)
        off = r0 - lo
        if off not in shifts:
            shifts[off] = jnp.where(u_i == src_time + off, 1.0, 0.0).astype(BF16)
        ys = []
        for c0 in range(0, 2 * dqk, dqk):
            cols = slice(c0, c0 + dqk)
            a = _dot(shifts[off], zm_ref[0, lo:lo + win, cols])
            ys.append(jnp.concatenate(
                [functools.reduce(jnp.add, [a[g + k * sub:g + (k + 1) * sub] * cw[k:k + 1, cols]
                                            for k in range(CONV_K)])
                 for g in range(0, CONV_K * rblk, CONV_K * sub)], axis=0))
        y = jnp.concatenate(ys, axis=1)
        y = y * jax.nn.sigmoid(y)
        q = y[:, :dqk].astype(BF16)
        kk = y[:, dqk:] * (M_DK ** -0.5)
        kt = kk.T
        kt_s[:, r0:r0 + rblk] = kt
        ktb_s[:, r0:r0 + rblk] = kt.astype(BF16)
        for h in range(M_HEADS):
            pair = q[:, (h // 2) * 2 * M_DK:(h // 2 + 1) * 2 * M_DK]
            keep = (lane_q >= M_DK) if (h % 2) else (lane_q < M_DK)
            qx_s[r0:r0 + rblk, h * 2 * M_DK:(h + 1) * 2 * M_DK] = jnp.where(keep, pair, jnp.zeros_like(pair))

    g = zgt_ref[...] + bg_ref[...]
    li = g[0:nseq] * LOG2E
    lf = _log_sigmoid(g[nseq:2 * nseq]) * LOG2E
    pos = lax.broadcasted_iota(jnp.int32, li.shape, 1) % chunk
    is_fw = lax.broadcasted_iota(jnp.int32, li.shape, 0) < M_HEADS
    b = jnp.where(is_fw, _chunk_scan(lf, jnp.add, 0.0, pos, chunk, False),
                  _chunk_scan(lf, jnp.add, 0.0, pos, chunk, True))
    c = li - b
    cmax = jnp.where(is_fw, _chunk_scan(c, jnp.maximum, -jnp.inf, pos, chunk, False),
                     _chunk_scan(c, jnp.maximum, -jnp.inf, pos, chunk, True))
    for j in range(nc):
        cols = slice(j * chunk, (j + 1) * chunk)
        brow_s[j] = b[:, cols]
        cmrow_s[j] = cmax[:, cols]
        crow_s[j] = c[:, cols]

    c_s[...] = jnp.zeros_like(c_s)
    m_s[...] = jnp.zeros_like(m_s)

    rr = lax.broadcasted_iota(jnp.int32, (chunk, chunk), 0)
    cc = lax.broadcasted_iota(jnp.int32, (chunk, chunk), 1)
    causal = (cc <= rr, cc >= rr)
    ones = jnp.ones((chunk, M_DV), BF16)

    def step(j, carry):
        for d in range(2):
            cj = j if d == 0 else nc - 1 - j
            rows = pl.ds(pl.multiple_of(cj * chunk, chunk), chunk)
            last = chunk - 1 if d == 0 else 0
            h_out = hf_ref if d == 0 else hb_ref
            for pr in range(M_HEADS // 2):
                state = c_s[d, pr]
                state_bf = state.astype(BF16)
                ktp = ktb_s[pr * 2 * M_DK:(pr + 1) * 2 * M_DK, rows]
                for half in range(2):
                    h = 2 * pr + half
                    i = d * M_HEADS + h
                    blk = slice(h * M_DV, (h + 1) * M_DV)
                    qc = qx_s[rows, blk]
                    v_aug = jnp.concatenate(
                        [zm_ref[0, rows, 2 * dqk + h * M_DV:2 * dqk + (h + 1) * M_DV], ones], axis=1)
                    crow = crow_s[cj, i:i + 1, :]
                    m_old = m_s[i:i + 1, :]
                    b_rep = _lane_replicated(brow_s[cj, i:i + 1, :])
                    m_rep = jnp.maximum(m_old, _lane_replicated(cmrow_s[cj, i:i + 1, :]))
                    dt = jnp.where(causal[d], jnp.exp2(crow - m_rep), 0.0)
                    sm = (_dot(qc, ktp) * dt).astype(BF16)
                    qw = (qc * jnp.exp2(m_old - m_rep)).astype(BF16)
                    tot = _dot(jnp.concatenate([qw, sm], axis=1),
                               jnp.concatenate([state_bf, v_aug], axis=0))
                    floor = jnp.exp2(-(b_rep + m_rep))
                    h_out[0, rows, blk] = tot[:, :M_DV] / jnp.maximum(jnp.abs(tot[:, M_DV:]), floor)
                    m_last = m_rep[last:last + 1, :]
                    wk = jnp.exp2(crow - m_last[:, 0:1])
                    kth = kt_s[h * M_DK:(h + 1) * M_DK, rows]
                    dc = _dot((kth * wk).astype(BF16), v_aug)
                    sub = slice(half * M_DK, (half + 1) * M_DK)
                    c_s[d, pr, sub, :] = jnp.exp2(m_old - m_last)[:, 0:1] * state[sub] + dc
                    m_s[i:i + 1, :] = b_rep[last:last + 1, :] + m_last
        return carry

    def steps(jj, carry):
        for k in range(MLSTM_CHUNKS_PER_STEP):
            step(jj * MLSTM_CHUNKS_PER_STEP + k, carry)
        return carry

    lax.fori_loop(0, nc // MLSTM_CHUNKS_PER_STEP, steps, 0)


def _mlstm_call(zm, zgt, conv_w, bg):
    b, s, wz = zm.shape
    chunk = MLSTM_CHUNK
    nc = s // chunk
    dv = M_HEADS * M_DV
    dqk = M_HEADS * M_DK
    nseq = 2 * M_HEADS
    full = lambda a: pl.BlockSpec(a.shape, lambda i: (0,) * a.ndim)
    per_b = lambda n: pl.BlockSpec((1, s, n), lambda i: (i, 0, 0))
    gate_rows = lambda: pltpu.VMEM((nc, nseq, chunk), F32)
    return pl.pallas_call(
        functools.partial(_mlstm_kernel, chunk=chunk, rblk=MLSTM_ROW_BLOCK),
        grid=(b,),
        in_specs=[per_b(wz), pl.BlockSpec((zgt.shape[0], s), lambda i: (0, i)), full(conv_w), full(bg)],
        out_specs=[per_b(dv), per_b(dv)],
        out_shape=[jax.ShapeDtypeStruct((b, s, dv), F32), jax.ShapeDtypeStruct((b, s, dv), F32)],
        scratch_shapes=[
            pltpu.VMEM((s, M_HEADS * 2 * M_DK), BF16),
            pltpu.VMEM((dqk, s), F32),
            pltpu.VMEM((dqk, s), BF16),
            gate_rows(),
            gate_rows(),
            gate_rows(),
            pltpu.VMEM((2, M_HEADS // 2, 2 * M_DK, 2 * M_DV), F32),
            pltpu.VMEM((nseq, LANES_V7X), F32),
        ],
        compiler_params=pltpu.CompilerParams(
            dimension_semantics=("arbitrary",), vmem_limit_bytes=VMEM_LIMIT_BYTES_V7X),
        name="mlstm_bidir",
    )(zm, zgt, conv_w, bg)


def _gate_rows(a16):
    hh = M_HEADS
    return jnp.concatenate([a16[..., 0:hh], a16[..., 2 * hh:3 * hh], a16[..., hh:2 * hh], a16[..., 3 * hh:]],
                           axis=-1)


def kernel(x, p, g_ffn1, w_ffn1_in, w_ffn1_out, g_mix, w_in, b_mgate, conv_w, g_mnorm, lam_q1, lam_k1, lam_q2, lam_k2, g_anorm, w_out, g_ffn2, w_ffn2_in, w_ffn2_out, g_ple, w_ple_gate, w_ple_proj, g_final):
    bsz, s, d = x.shape
    depth = w_in.shape[0]
    t = bsz * s
    m_qk, m_v, m_g = M_HEADS * M_DK, M_HEADS * M_DV, 4 * M_HEADS
    a_qk = A_HEADS * 2 * A_DK
    m_end = 2 * m_qk + 2 * m_v
    assert w_in.shape[2] == m_end + m_g + 2 * a_qk + A_HEADS * A_DV and conv_w.shape[1:] == (CONV_K, 2 * m_qk)
    assert t % TOKEN_TILE == 0 and w_ffn1_out.shape[1] % FF_CHUNK == 0 and w_ffn2_out.shape[1] % FF_CHUNK == 0
    assert s % (MLSTM_CHUNK * MLSTM_CHUNKS_PER_STEP) == 0 and s % MLSTM_ROW_BLOCK == 0
    assert s % ATTN_Q_TILE == 0 and ATTN_Q_TILE % ATTN_KEY_BLOCK == 0
    slopes = np.array([2.0 ** (-8.0 * (h + 1) / A_HEADS) for h in range(A_HEADS)], dtype=np.float32)
    slopes = jnp.broadcast_to(jnp.asarray(slopes * np.float32(LOG2E))[:, None, None],
                              (A_HEADS, 1, LANES_V7X))
    row = lambda v: v.reshape(1, -1).astype(F32)

    h = x.reshape(t, d)
    for i in range(depth):
        lam_init = 0.8 - 0.6 * math.exp(-0.3 * i)
        wi = w_in[i]
        wm = wi[:, :m_end].astype(BF16)
        wa = wi[:, m_end + m_g:]
        wa = jnp.concatenate([wa[:, :a_qk] * (A_DK ** -0.5 * LOG2E), wa[:, a_qk:]], axis=1).astype(BF16)
        wgt = _gate_rows(wi[:, m_end:m_end + m_g]).T.astype(BF16)
        bg = _gate_rows(b_mgate[i].astype(F32)).reshape(m_g, 1)

        (h1, zm, za, vt, zgt), (win2, wo2, wout, wpg, wpp) = _pre_call(
            h, row(g_ffn1[i]), w_ffn1_in[i].astype(BF16), w_ffn1_out[i].astype(BF16), row(g_mix[i]), wm, wa, wgt,
            i, (w_ffn2_in, w_ffn2_out, w_out, w_ple_gate, w_ple_proj))
        hf, hb = _mlstm_call(zm.reshape(bsz, s, -1), zgt, conv_w[i].astype(F32), bg)
        lam_vec = jnp.stack([lam_q1[i], lam_k1[i], lam_q2[i], lam_k2[i]]).astype(F32)
        ya = _attn_call(za.reshape(bsz, s, -1), vt, lam_vec, slopes,
                        g_anorm[i].reshape(A_HEADS, 1, A_DV).astype(F32), lam_init)
        assert (2 * m_qk + m_v) % m_v == 0
        h = _post_call(h1, hf.reshape(t, -1), hb.reshape(t, -1), zm, (2 * m_qk + m_v) // m_v, ya.reshape(t, -1),
                       p[i].reshape(t, -1), row(g_mnorm[i]),
                       wout, row(g_ffn2[i]), win2, wo2, row(g_ple[i]), wpg, wpp, row(g_final),
                       final_norm=(i == depth - 1))
    return h.reshape(bsz, s, d)
```

```python
import functools
import math

import jax
import jax.numpy as jnp
import numpy as np
from jax import lax
from jax.experimental import pallas as pl
from jax.experimental.pallas import tpu as pltpu

F32 = jnp.float32
BF16 = jnp.bfloat16

EPS = 1e-6
LOG2E = 1.4426950408889634

M_HEADS = 4
M_DK = 64
M_DV = 128
A_HEADS = 4
A_DK = 64
A_DV = 128
CONV_K = 3

LANES_V7X = 128
SUBLANES_V7X = 8
BF16_ROWS_PER_VREG_V7X = 16
VMEM_LIMIT_BYTES_V7X = 56 * 1024 * 1024

TOKEN_TILE = 512
FF_CHUNK = 256
MLSTM_CHUNK = 128
MLSTM_CHUNKS_PER_STEP = 8
MLSTM_ROW_BLOCK = 128
MLSTM_GATE_LANES = 8192
ATTN_Q_TILE = 256
ATTN_KEY_BLOCK = 256


def _aligned(x, m):
    return x if isinstance(x, int) else pl.multiple_of(x, m)


def _dot(a, b):
    return jnp.dot(a, b, preferred_element_type=F32)


def _dot_nt(a, b):
    return lax.dot_general(a, b, (((1,), (1,)), ((), ())), preferred_element_type=F32)


def _rmsnorm(x, g):
    ms = jnp.mean(x * x, axis=-1, keepdims=True)
    return x * lax.rsqrt(ms + EPS) * g


def _swiglu(xn, win_ref, wo_ref):
    f = wo_ref.shape[0]
    parts = []
    for c0 in range(0, f, FF_CHUNK):
        g = _dot(xn, win_ref[:, c0:c0 + FF_CHUNK])
        u = _dot(xn, win_ref[:, f + c0:f + c0 + FF_CHUNK])
        parts.append((g * jax.nn.sigmoid(g) * u).astype(BF16))
    return _dot(jnp.concatenate(parts, axis=1), wo_ref[...])


def _resident(shape):
    nd = len(shape)
    return pl.BlockSpec(shape, lambda *_: (0,) * nd, pipeline_mode=pl.Buffered(1))


def _pre_kernel(x_ref, g1_ref, win_ref, wo_ref, gm_ref, wm_ref, wa_ref, wgt_ref, *rest):
    n_cast = (len(rest) - 5) // 2
    cast_in, (h_ref, zm_ref, za_ref, vt_ref, zgt_ref), cast_out = (
        rest[:n_cast], rest[n_cast:n_cast + 5], rest[n_cast + 5:])
    x = x_ref[...]
    xn = _rmsnorm(x, g1_ref[...]).astype(BF16)
    h = x + 0.5 * _swiglu(xn, win_ref, wo_ref)
    h_ref[...] = h
    u = _rmsnorm(h, gm_ref[...]).astype(BF16)
    zm_ref[...] = _dot(u, wm_ref[...]).astype(BF16)
    za = _dot(u, wa_ref[...])
    nqk = za_ref.shape[1]
    za_ref[...] = za[:, :nqk].astype(BF16)
    vt = za[:, nqk:].T.astype(BF16)
    one = jnp.ones((BF16_ROWS_PER_VREG_V7X, vt.shape[1]), BF16)
    vt_ref[...] = jnp.concatenate(
        [blk for c0 in range(0, vt.shape[0], A_DV) for blk in (vt[c0:c0 + A_DV], one)], axis=0)
    zgt_ref[...] = _dot_nt(wgt_ref[...], u)
    for src, dst in zip(cast_in, cast_out):
        dst[...] = src[0].astype(BF16)


def _slab_rows(rows, steps):
    for r in range(BF16_ROWS_PER_VREG_V7X, rows + 1, BF16_ROWS_PER_VREG_V7X):
        if rows % r == 0 and rows // r <= steps:
            return r
    raise ValueError((rows, steps))


def _pre_call(h, g1, win, wo, gm, wm, wa, wgt, layer, later_weights):
    t, d = h.shape
    tm = TOKEN_TILE
    ng = wgt.shape[0]
    n_av = A_HEADS * A_DV
    n_vt = A_HEADS * (A_DV + BF16_ROWS_PER_VREG_V7X)
    steps = t // tm
    row = lambda n: pl.BlockSpec((tm, n), lambda i: (i, 0))
    weights = (g1, win, wo, gm, wm, wa, wgt)
    slabs_in, slabs_out = [], []
    for w in later_weights:
        r = _slab_rows(w.shape[1], steps)
        last = w.shape[1] // r - 1
        slabs_in.append(pl.BlockSpec((1, r, w.shape[2]), lambda i, last=last: (layer, jnp.minimum(i, last), 0)))
        slabs_out.append(pl.BlockSpec((r, w.shape[2]), lambda i, last=last: (jnp.minimum(i, last), 0)))
    outs = pl.pallas_call(
        _pre_kernel,
        grid=(steps,),
        in_specs=[row(d)] + [_resident(w.shape) for w in weights] + slabs_in,
        out_specs=[row(d), row(wm.shape[1]), row(wa.shape[1] - n_av),
                   pl.BlockSpec((n_vt, tm), lambda i: (0, i)), pl.BlockSpec((ng, tm), lambda i: (0, i))]
                  + slabs_out,
        out_shape=[jax.ShapeDtypeStruct((t, d), F32),
                   jax.ShapeDtypeStruct((t, wm.shape[1]), BF16),
                   jax.ShapeDtypeStruct((t, wa.shape[1] - n_av), BF16),
                   jax.ShapeDtypeStruct((n_vt, t), BF16),
                   jax.ShapeDtypeStruct((ng, t), F32)]
                  + [jax.ShapeDtypeStruct(w.shape[1:], BF16) for w in later_weights],
        compiler_params=pltpu.CompilerParams(
            dimension_semantics=("arbitrary",), vmem_limit_bytes=VMEM_LIMIT_BYTES_V7X),
        name="pre_ffn_inproj",
    )(h, *weights, *later_weights)
    return outs[:5], outs[5:]


def _post_kernel(h_ref, hf_ref, hb_ref, og_ref, ya_ref, p_ref, gn_ref, wout_ref, g2_ref, win_ref, wo_ref,
                 gp_ref, wpg_ref, wpp_ref, gf_ref, o_ref, *, final_norm):
    dm = hf_ref.shape[1]
    h = h_ref[...] + _dot(ya_ref[...], wout_ref[dm:, :])
    heads = []
    for c0 in range(0, dm, M_DV):
        hh = hf_ref[:, c0:c0 + M_DV] + hb_ref[:, c0:c0 + M_DV]
        hn = hh * lax.rsqrt(jnp.mean(hh * hh, axis=-1, keepdims=True) + EPS) * gn_ref[:, c0:c0 + M_DV]
        heads.append((jax.nn.sigmoid(og_ref[:, c0:c0 + M_DV].astype(F32)) * hn).astype(BF16))
    h = h + _dot(jnp.concatenate(heads, axis=1), wout_ref[0:dm, :])
    xn = _rmsnorm(h, g2_ref[...]).astype(BF16)
    h = h + 0.5 * _swiglu(xn, win_ref, wo_ref)
    gate = jax.nn.sigmoid(_dot(_rmsnorm(h, gp_ref[...]).astype(BF16), wpg_ref[...]))
    h = h + gate * _dot(p_ref[...].astype(BF16), wpp_ref[...])
    if final_norm:
        h = _rmsnorm(h, gf_ref[...])
    o_ref[...] = h


def _post_call(h1, hf, hb, zm, og_block, ya, p, gn, wout, g2, win, wo, gp, wpg, wpp, gf, final_norm):
    t, d = h1.shape
    tm = TOKEN_TILE
    row = lambda n: pl.BlockSpec((tm, n), lambda i: (i, 0))
    weights = (gn, wout, g2, win, wo, gp, wpg, wpp, gf)
    return pl.pallas_call(
        functools.partial(_post_kernel, final_norm=final_norm),
        grid=(t // tm,),
        in_specs=[row(d), row(hf.shape[1]), row(hb.shape[1]),
                  pl.BlockSpec((tm, hf.shape[1]), lambda i: (i, og_block)), row(ya.shape[1]), row(p.shape[1])]
                 + [_resident(w.shape) for w in weights],
        out_specs=row(d),
        out_shape=jax.ShapeDtypeStruct((t, d), F32),
        compiler_params=pltpu.CompilerParams(
            dimension_semantics=("arbitrary",), vmem_limit_bytes=VMEM_LIMIT_BYTES_V7X),
        name="post_outproj_ffn_ple",
    )(h1, hf, hb, zm, ya, p, *weights)


def _attn_kernel(lam_ref, slope_ref, g_ref, q_ref, k_ref, vt_ref, qn_ref, kn_ref, o_ref,
                 tab_s, st0_s, st1_s, m0_s, *, lam_init, tq, kblk):
    s = k_ref.shape[1]
    nq = s // tq
    nkb = s // kblk
    first_row = pl.program_id(1) == 0

    @pl.when(first_row)
    def _():
        r = lax.broadcasted_iota(jnp.int32, tab_s.shape, 1)
        c = lax.broadcasted_iota(jnp.int32, tab_s.shape, 0)
        tab_s[...] = jnp.abs(r - c + (s - tq)).astype(F32) * slope_ref[0, :, 0:1]

    lam_v = lam_ref[...]
    lam = (jnp.exp(jnp.sum(lam_v[0:1] * lam_v[1:2], axis=1, keepdims=True))
           - jnp.exp(jnp.sum(lam_v[2:3] * lam_v[3:4], axis=1, keepdims=True)) + lam_init)

    gain = g_ref[0] * (1.0 - lam_init)
    lane = lax.broadcasted_iota(jnp.int32, (tq, 2 * A_DK), 1)
    neg = jnp.full((1, 2 * tq), -jnp.inf, F32)
    acc0 = jnp.zeros((vt_ref.shape[0], 2 * tq), F32)

    def stage(tile, q, keys_ref, dst, m_prev, src):
        zero = jnp.zeros_like(q)
        qm = jnp.concatenate([jnp.where(lane < A_DK, q, zero), jnp.where(lane >= A_DK, q, zero)], axis=0)
        mx, acc = neg, acc0
        for kb in range(nkb):
            rows = slice(kb * kblk, (kb + 1) * kblk)
            bias = tab_s[pl.ds(_aligned((nq - 1 - tile) * tq + kb * kblk, kblk), kblk), :]
            sc = _dot_nt(keys_ref[0, rows, :], qm) - jnp.concatenate([bias, bias], axis=1)
            dst[rows, :] = sc
            mx = jnp.maximum(mx, jnp.max(sc, axis=0, keepdims=True))
            if src is not None:
                e = jnp.exp2(src[rows, :] - m_prev).astype(BF16)
                acc = acc + _dot(vt_ref[:, rows], e)
        return mx, acc

    def finish(u, acc):
        r = 1.0 / acc[A_DV:A_DV + 1, :]
        o = acc[:A_DV, :tq] * r[:, :tq] - lam * (acc[:A_DV, tq:] * r[:, tq:])
        on = o * lax.rsqrt(jnp.mean(o * o, axis=0, keepdims=True) + EPS)
        o_ref[0, pl.ds(_aligned(u * tq, tq), tq), :] = (on.T * gain).astype(o_ref.dtype)

    assert nq % 2 == 0 and nq >= 4

    def q_tile(u):
        return q_ref[0, pl.ds(_aligned(u * tq, tq), tq), :]

    @pl.when(first_row)
    def _():
        m0_s[...] = stage(0, q_tile(0), k_ref, st0_s, None, None)[0]

    def pair(j, carry, last):
        m_even, acc_odd = carry
        if acc_odd is not None:
            finish(2 * j - 1, acc_odd)
        m_odd, acc_even = stage(2 * j + 1, q_tile(2 * j + 1), k_ref, st1_s, m_even, st0_s)
        finish(2 * j, acc_even)
        if last:
            return stage(0, qn_ref[0], kn_ref, st0_s, m_odd, st1_s)
        return stage(2 * j + 2, q_tile(2 * j + 2), k_ref, st0_s, m_odd, st1_s)

    carry = pair(0, (m0_s[...], None), False)
    carry = lax.fori_loop(1, nq // 2 - 1, lambda j, c: pair(j, c, False), carry)
    m_next, acc_last = pair(nq // 2 - 1, carry, True)
    finish(nq - 1, acc_last)
    m0_s[...] = m_next


def _attn_call(za, vt, lam_vec, slopes, g_anorm, lam_init):
    b, s, _ = za.shape
    tq = ATTN_Q_TILE
    blk = 2 * A_DK
    assert blk == A_DV == LANES_V7X
    head_block = lambda off: pl.BlockSpec((1, s, blk), lambda h, i: (i, 0, off + h))
    nxt = lambda i: jnp.minimum(i + 1, b - 1)
    return pl.pallas_call(
        functools.partial(_attn_kernel, lam_init=lam_init, tq=tq, kblk=ATTN_KEY_BLOCK),
        grid=(A_HEADS, b),
        in_specs=[pl.BlockSpec(lam_vec.shape, lambda h, i: (0, 0)),
                  pl.BlockSpec((1, 1, LANES_V7X), lambda h, i: (h, 0, 0)),
                  pl.BlockSpec((1, 1, A_DV), lambda h, i: (h, 0, 0)),
                  head_block(0), head_block(A_HEADS),
                  pl.BlockSpec((A_DV + BF16_ROWS_PER_VREG_V7X, s), lambda h, i: (h, i)),
                  pl.BlockSpec((1, tq, blk), lambda h, i: (nxt(i), 0, h)),
                  pl.BlockSpec((1, s, blk), lambda h, i: (nxt(i), 0, A_HEADS + h))],
        out_specs=pl.BlockSpec((1, s, A_DV), lambda h, i: (i, 0, h)),
        out_shape=jax.ShapeDtypeStruct((b, s, A_HEADS * A_DV), BF16),
        scratch_shapes=[pltpu.VMEM((2 * s - tq, tq), F32),
                        pltpu.VMEM((s, 2 * tq), F32),
                        pltpu.VMEM((s, 2 * tq), F32),
                        pltpu.VMEM((1, 2 * tq), F32)],
        compiler_params=pltpu.CompilerParams(
            dimension_semantics=("arbitrary", "arbitrary"), vmem_limit_bytes=VMEM_LIMIT_BYTES_V7X),
        name="diff_attention",
    )(lam_vec, slopes, g_anorm, za, za, vt, za, za)


def _log_sigmoid(x):
    return jnp.minimum(x, 0.0) - jnp.log(1.0 + jnp.exp(-jnp.abs(x)))


def _chunk_scan(x, op, fill, pos_in_chunk, chunk, reverse):
    n = x.shape[1]
    k = 1
    while k < chunk:
        if reverse:
            shifted = pltpu.roll(x, n - k, 1)
            valid = pos_in_chunk < chunk - k
        else:
            shifted = pltpu.roll(x, k, 1)
            valid = pos_in_chunk >= k
        x = op(x, jnp.where(valid, shifted, fill))
        k *= 2
    return x


def _lane_replicated(row):
    return jnp.broadcast_to(row, (LANES_V7X, row.shape[1])).T


def _mlstm_gate_rows(g, chunk):
    nseq = 2 * M_HEADS
    li = g[0:nseq] * LOG2E
    lf = _log_sigmoid(g[nseq:2 * nseq]) * LOG2E
    pos = lax.broadcasted_iota(jnp.int32, li.shape, 1) % chunk
    is_fw = lax.broadcasted_iota(jnp.int32, li.shape, 0) < M_HEADS
    b = jnp.where(is_fw, _chunk_scan(lf, jnp.add, 0.0, pos, chunk, False),
                  _chunk_scan(lf, jnp.add, 0.0, pos, chunk, True))
    c = li - b
    cmax = jnp.where(is_fw, _chunk_scan(c, jnp.maximum, -jnp.inf, pos, chunk, False),
                     _chunk_scan(c, jnp.maximum, -jnp.inf, pos, chunk, True))
    return jnp.concatenate([b, cmax, c], axis=0)


def _gates_kernel(zgt_ref, bg_ref, o_ref, *, chunk):
    o_ref[...] = _mlstm_gate_rows(zgt_ref[...] + bg_ref[...], chunk)


def _gates_call(zgt, bg):
    ng, t = zgt.shape
    blk = MLSTM_GATE_LANES
    nout = 3 * 2 * M_HEADS
    assert t % blk == 0 and blk % MLSTM_CHUNK == 0
    return pl.pallas_call(
        functools.partial(_gates_kernel, chunk=MLSTM_CHUNK),
        grid=(t // blk,),
        in_specs=[pl.BlockSpec((ng, blk), lambda i: (0, i)), pl.BlockSpec(bg.shape, lambda i: (0, 0))],
        out_specs=pl.BlockSpec((nout, blk), lambda i: (0, i)),
        out_shape=jax.ShapeDtypeStruct((nout, t), F32),
        compiler_params=pltpu.CompilerParams(
            dimension_semantics=("arbitrary",), vmem_limit_bytes=VMEM_LIMIT_BYTES_V7X),
        name="mlstm_gates",
    )(zgt, bg)


def _mlstm_kernel(zm_ref, zgt_ref, convw_ref, hf_ref, hb_ref,
                  qx_s, kt_s, ktb_s, brow_s, cmrow_s, crow_s, c_s, m_s, *, chunk, rblk):
    s = zm_ref.shape[1]
    nc = s // chunk
    nseq = 2 * M_HEADS
    dqk = M_HEADS * M_DK

    cw = convw_ref[...]
    lane_q = lax.broadcasted_iota(jnp.int32, (rblk, 2 * M_DK), 1)
    win = 2 * rblk
    sub = SUBLANES_V7X
    rho = lax.broadcasted_iota(jnp.int32, (CONV_K * rblk, win), 0)
    u_i = lax.broadcasted_iota(jnp.int32, (CONV_K * rblk, win), 1)
    src_time = (rho // (CONV_K * sub)) * sub + rho % sub + (rho % (CONV_K * sub)) // sub - CONV_K // 2
    shifts = {}
    for j in range(s // rblk):
        r0 = j * rblk
        lo = min(max(r0 - rblk // 2, 0), s - win)
        off = r0 - lo
        if off not in shifts:
            shifts[off] = jnp.where(u_i == src_time + off, 1.0, 0.0).astype(BF16)
        ys = []
        for c0 in range(0, 2 * dqk, dqk):
            cols = slice(c0, c0 + dqk)
            a = _dot(shifts[off], zm_ref[0, lo:lo + win, cols])
            ys.append(jnp.concatenate(
                [functools.reduce(jnp.add, [a[g + k * sub:g + (k + 1) * sub] * cw[k:k + 1, cols]
                                            for k in range(CONV_K)])
                 for g in range(0, CONV_K * rblk, CONV_K * sub)], axis=0))
        y = jnp.concatenate(ys, axis=1)
        y = y * jax.nn.sigmoid(y)
        q = y[:, :dqk].astype(BF16)
        kk = y[:, dqk:] * (M_DK ** -0.5)
        kt = kk.T
        kt_s[:, r0:r0 + rblk] = kt
        ktb_s[:, r0:r0 + rblk] = kt.astype(BF16)
        for h in range(M_HEADS):
            pair = q[:, (h // 2) * 2 * M_DK:(h // 2 + 1) * 2 * M_DK]
            keep = (lane_q >= M_DK) if (h % 2) else (lane_q < M_DK)
            qx_s[r0:r0 + rblk, h * 2 * M_DK:(h + 1) * 2 * M_DK] = jnp.where(keep, pair, jnp.zeros_like(pair))

    for j in range(nc):
        cols = slice(j * chunk, (j + 1) * chunk)
        brow_s[j] = zgt_ref[0:nseq, cols]
        cmrow_s[j] = zgt_ref[nseq:2 * nseq, cols]
        crow_s[j] = zgt_ref[2 * nseq:3 * nseq, cols]

    c_s[...] = jnp.zeros_like(c_s)
    m_s[...] = jnp.zeros_like(m_s)

    rr = lax.broadcasted_iota(jnp.int32, (chunk, chunk), 0)
    cc = lax.broadcasted_iota(jnp.int32, (chunk, chunk), 1)
    causal = (cc <= rr, cc >= rr)
    ones = jnp.ones((chunk, M_DV), BF16)

    def step(j, carry):
        for d in range(2):
            cj = j if d == 0 else nc - 1 - j
            rows = pl.ds(pl.multiple_of(cj * chunk, chunk), chunk)
            last = chunk - 1 if d == 0 else 0
            h_out = hf_ref if d == 0 else hb_ref
            for pr in range(M_HEADS // 2):
                state = c_s[d, pr]
                state_bf = state.astype(BF16)
                ktp = ktb_s[pr * 2 * M_DK:(pr + 1) * 2 * M_DK, rows]
                for half in range(2):
                    h = 2 * pr + half
                    i = d * M_HEADS + h
                    blk = slice(h * M_DV, (h + 1) * M_DV)
                    qc = qx_s[rows, blk]
                    v_aug = jnp.concatenate(
                        [zm_ref[0, rows, 2 * dqk + h * M_DV:2 * dqk + (h + 1) * M_DV], ones], axis=1)
                    crow = crow_s[cj, i:i + 1, :]
                    m_old = m_s[i:i + 1, :]
                    b_rep = _lane_replicated(brow_s[cj, i:i + 1, :])
                    m_rep = jnp.maximum(m_old, _lane_replicated(cmrow_s[cj, i:i + 1, :]))
                    dt = jnp.where(causal[d], jnp.exp2(crow - m_rep), 0.0)
                    sm = (_dot(qc, ktp) * dt).astype(BF16)
                    qw = (qc * jnp.exp2(m_old - m_rep)).astype(BF16)
                    tot = _dot(jnp.concatenate([qw, sm], axis=1),
                               jnp.concatenate([state_bf, v_aug], axis=0))
                    floor = jnp.exp2(-(b_rep + m_rep))
                    h_out[0, rows, blk] = tot[:, :M_DV] / jnp.maximum(jnp.abs(tot[:, M_DV:]), floor)
                    m_last = m_rep[last:last + 1, :]
                    wk = jnp.exp2(crow - m_last[:, 0:1])
                    kth = kt_s[h * M_DK:(h + 1) * M_DK, rows]
                    dc = _dot((kth * wk).astype(BF16), v_aug)
                    sub = slice(half * M_DK, (half + 1) * M_DK)
                    c_s[d, pr, sub, :] = jnp.exp2(m_old - m_last)[:, 0:1] * state[sub] + dc
                    m_s[i:i + 1, :] = b_rep[last:last + 1, :] + m_last
        return carry

    def steps(jj, carry):
        for k in range(MLSTM_CHUNKS_PER_STEP):
            step(jj * MLSTM_CHUNKS_PER_STEP + k, carry)
        return carry

    lax.fori_loop(0, nc // MLSTM_CHUNKS_PER_STEP, steps, 0)


def _mlstm_call(zm, zgt, conv_w):
    b, s, wz = zm.shape
    chunk = MLSTM_CHUNK
    nc = s // chunk
    dv = M_HEADS * M_DV
    dqk = M_HEADS * M_DK
    nseq = 2 * M_HEADS
    full = lambda a: pl.BlockSpec(a.shape, lambda i: (0,) * a.ndim)
    per_b = lambda n: pl.BlockSpec((1, s, n), lambda i: (i, 0, 0))
    gate_rows = lambda: pltpu.VMEM((nc, nseq, chunk), F32)
    return pl.pallas_call(
        functools.partial(_mlstm_kernel, chunk=chunk, rblk=MLSTM_ROW_BLOCK),
        grid=(b,),
        in_specs=[per_b(wz), pl.BlockSpec((zgt.shape[0], s), lambda i: (0, i)), full(conv_w)],
        out_specs=[per_b(dv), per_b(dv)],
        out_shape=[jax.ShapeDtypeStruct((b, s, dv), F32), jax.ShapeDtypeStruct((b, s, dv), F32)],
        scratch_shapes=[
            pltpu.VMEM((s, M_HEADS * 2 * M_DK), BF16),
            pltpu.VMEM((dqk, s), F32),
            pltpu.VMEM((dqk, s), BF16),
            gate_rows(),
            gate_rows(),
            gate_rows(),
            pltpu.VMEM((2, M_HEADS // 2, 2 * M_DK, 2 * M_DV), F32),
            pltpu.VMEM((nseq, LANES_V7X), F32),
        ],
        compiler_params=pltpu.CompilerParams(
            dimension_semantics=("arbitrary",), vmem_limit_bytes=VMEM_LIMIT_BYTES_V7X),
        name="mlstm_bidir",
    )(zm, zgt, conv_w)


def _gate_rows(a16):
    hh = M_HEADS
    return jnp.concatenate([a16[..., 0:hh], a16[..., 2 * hh:3 * hh], a16[..., hh:2 * hh], a16[..., 3 * hh:]],
                           axis=-1)


def kernel(x, p, g_ffn1, w_ffn1_in, w_ffn1_out, g_mix, w_in, b_mgate, conv_w, g_mnorm, lam_q1, lam_k1, lam_q2, lam_k2, g_anorm, w_out, g_ffn2, w_ffn2_in, w_ffn2_out, g_ple, w_ple_gate, w_ple_proj, g_final):
    bsz, s, d = x.shape
    depth = w_in.shape[0]
    t = bsz * s
    m_qk, m_v, m_g = M_HEADS * M_DK, M_HEADS * M_DV, 4 * M_HEADS
    a_qk = A_HEADS * 2 * A_DK
    m_end = 2 * m_qk + 2 * m_v
    assert w_in.shape[2] == m_end + m_g + 2 * a_qk + A_HEADS * A_DV and conv_w.shape[1:] == (CONV_K, 2 * m_qk)
    assert t % TOKEN_TILE == 0 and w_ffn1_out.shape[1] % FF_CHUNK == 0 and w_ffn2_out.shape[1] % FF_CHUNK == 0
    assert s % (MLSTM_CHUNK * MLSTM_CHUNKS_PER_STEP) == 0 and s % MLSTM_ROW_BLOCK == 0
    assert s % ATTN_Q_TILE == 0 and ATTN_Q_TILE % ATTN_KEY_BLOCK == 0
    slopes = np.array([2.0 ** (-8.0 * (h + 1) / A_HEADS) for h in range(A_HEADS)], dtype=np.float32)
    slopes = jnp.broadcast_to(jnp.asarray(slopes * np.float32(LOG2E))[:, None, None],
                              (A_HEADS, 1, LANES_V7X))
    row = lambda v: v.reshape(1, -1).astype(F32)

    h = x.reshape(t, d)
    for i in range(depth):
        lam_init = 0.8 - 0.6 * math.exp(-0.3 * i)
        wi = w_in[i]
        wm = wi[:, :m_end].astype(BF16)
        wa = wi[:, m_end + m_g:]
        wa = jnp.concatenate([wa[:, :a_qk] * (A_DK ** -0.5 * LOG2E), wa[:, a_qk:]], axis=1).astype(BF16)
        wgt = _gate_rows(wi[:, m_end:m_end + m_g]).T.astype(BF16)
        bg = _gate_rows(b_mgate[i].astype(F32)).reshape(m_g, 1)

        (h1, zm, za, vt, zgt), (win2, wo2, wout, wpg, wpp) = _pre_call(
            h, row(g_ffn1[i]), w_ffn1_in[i].astype(BF16), w_ffn1_out[i].astype(BF16), row(g_mix[i]), wm, wa, wgt,
            i, (w_ffn2_in, w_ffn2_out, w_out, w_ple_gate, w_ple_proj))
        hf, hb = _mlstm_call(zm.reshape(bsz, s, -1), _gates_call(zgt, bg), conv_w[i].astype(F32))
        lam_vec = jnp.stack([lam_q1[i], lam_k1[i], lam_q2[i], lam_k2[i]]).astype(F32)
        ya = _attn_call(za.reshape(bsz, s, -1), vt, lam_vec, slopes,
                        g_anorm[i].reshape(A_HEADS, 1, A_DV).astype(F32), lam_init)
        assert (2 * m_qk + m_v) % m_v == 0
        h = _post_call(h1, hf.reshape(t, -1), hb.reshape(t, -1), zm, (2 * m_qk + m_v) // m_v, ya.reshape(t, -1),
                       p[i].reshape(t, -1), row(g_mnorm[i]),
                       wout, row(g_ffn2[i]), win2, wo2, row(g_ple[i]), wpg, wpp, row(g_final),
                       final_norm=(i == depth - 1))
    return h.reshape(bsz, s, d)
```
